```python
import math
import jax, jax.numpy as jnp
from jax import lax
import numpy as np

D_MODEL = 1024
BATCH = 4
SEQ = 4096
DEPTH = 4

N_HEADS = 8
HEAD_DIM = 64
ATTN_WIDTH = N_HEADS * HEAD_DIM
ROT_DIM = HEAD_DIM // 4
ROPE_THETA = 500000.0
MOBA_BLOCK = 256
MOBA_TOPK = 3
Q_CHUNK = 32
CONV_WIDTH = D_MODEL // 2
CONV_KERNEL = 31
D_FF = 4 * D_MODEL
IN_COLS = 3 * ATTN_WIDTH + 2 * CONV_WIDTH + 2 * D_MODEL
DN_ALPHA = (2 * DEPTH) ** 0.25
DN_BETA = (8 * DEPTH) ** -0.25
ADA_SCALE = 0.1
LN_EPS = 1e-5
NEG_INF = -1e30

kernel_name = "hybrid_moba_conformer_gated_deepnorm"


def layer_norm(x, g, b):
    xf = x.astype(jnp.float32)
    mu = jnp.mean(xf, -1, keepdims=True)
    var = jnp.mean(jnp.square(xf - mu), -1, keepdims=True)
    return ((xf - mu) * lax.rsqrt(var + LN_EPS)).astype(x.dtype) * g + b


def partial_rope(x, cos, sin):
    half = ROT_DIM // 2
    x1, x2, rest = x[..., :half], x[..., half:ROT_DIM], x[..., ROT_DIM:]
    return jnp.concatenate([x1 * cos - x2 * sin, x2 * cos + x1 * sin, rest], -1)


def moba_attention(q, k, v):
    B, H, S, hd = q.shape
    n_blk = -(-S // MOBA_BLOCK)
    pad = n_blk * MOBA_BLOCK - S
    kp = jnp.pad(k, ((0, 0), (0, 0), (0, pad), (0, 0)))
    vp = jnp.pad(v, ((0, 0), (0, 0), (0, pad), (0, 0)))
    kb = kp.reshape(B, H, n_blk, MOBA_BLOCK, hd)
    vb = vp.reshape(B, H, n_blk, MOBA_BLOCK, hd)
    k_mean = jnp.mean(kb.astype(jnp.float32), axis=3).astype(q.dtype)
    topk = min(MOBA_TOPK, n_blk)
    scale = hd ** -0.5
    n_chunks = S // Q_CHUNK
    q_chunks = jnp.moveaxis(q.reshape(B, H, n_chunks, Q_CHUNK, hd), 2, 0)
    b_idx = jnp.arange(B)[:, None, None, None]
    h_idx = jnp.arange(H)[None, :, None, None]
    blk_ids = jnp.arange(n_blk)

    def attend_chunk(args):
        ci, qc = args
        q_pos = ci * Q_CHUNK + jnp.arange(Q_CHUNK)
        own = (ci * Q_CHUNK) // MOBA_BLOCK
        gate = jnp.einsum('bhqd,bhnd->bhqn', qc, k_mean, preferred_element_type=jnp.float32)
        gate = jnp.where(blk_ids < own, gate, NEG_INF)
        _, sel = lax.top_k(gate, topk)
        sel_ok = jnp.arange(topk) < own
        k_sel = kb[b_idx, h_idx, sel]
        v_sel = vb[b_idx, h_idx, sel]
        s_sel = jnp.einsum('bhqd,bhqjtd->bhqjt', qc, k_sel, preferred_element_type=jnp.float32) * scale
        s_sel = jnp.where(sel_ok[:, None], s_sel, NEG_INF)
        k_own = lax.dynamic_slice_in_dim(kp, own * MOBA_BLOCK, MOBA_BLOCK, axis=2)
        v_own = lax.dynamic_slice_in_dim(vp, own * MOBA_BLOCK, MOBA_BLOCK, axis=2)
        s_own = jnp.einsum('bhqd,bhtd->bhqt', qc, k_own, preferred_element_type=jnp.float32) * scale
        k_pos = own * MOBA_BLOCK + jnp.arange(MOBA_BLOCK)
        s_own = jnp.where(k_pos[None, :] <= q_pos[:, None], s_own, NEG_INF)
        logits = jnp.concatenate([s_sel.reshape(B, H, Q_CHUNK, topk * MOBA_BLOCK), s_own], -1)
        p = jax.nn.softmax(logits, axis=-1).astype(v.dtype)
        p_sel = p[..., :topk * MOBA_BLOCK].reshape(B, H, Q_CHUNK, topk, MOBA_BLOCK)
        p_own = p[..., topk * MOBA_BLOCK:]
        return (jnp.einsum('bhqjt,bhqjtd->bhqd', p_sel, v_sel)
                + jnp.einsum('bhqt,bhtd->bhqd', p_own, v_own))

    out = lax.map(attend_chunk, (jnp.arange(n_chunks), q_chunks))
    return jnp.moveaxis(out, 0, 2).reshape(B, H, S, hd)


def conformer_conv(glu_in, dw_w, dw_b, ln_g, ln_b, w_pw):
    a, g = jnp.split(glu_in, 2, axis=-1)
    h = a * jax.nn.sigmoid(g)
    h = lax.conv_general_dilated(h, dw_w[:, None, :], window_strides=(1,),
                                 padding=[(CONV_KERNEL - 1, 0)],
                                 dimension_numbers=('NWC', 'WIO', 'NWC'),
                                 feature_group_count=CONV_WIDTH) + dw_b
    h = jax.nn.silu(layer_norm(h, ln_g, ln_b))
    return h @ w_pw


def mixer(u, cos, sin, w_in, w_attn_br, dw_w, dw_b, cln_g, cln_b, w_conv_pw, w_out):
    B, S, _ = u.shape
    proj = u @ w_in
    splits = [ATTN_WIDTH, 2 * ATTN_WIDTH, 3 * ATTN_WIDTH,
              3 * ATTN_WIDTH + 2 * CONV_WIDTH, 3 * ATTN_WIDTH + 2 * CONV_WIDTH + D_MODEL]
    q, k, v, glu_in, g_attn, g_conv = jnp.split(proj, splits, axis=-1)
    q = partial_rope(q.reshape(B, S, N_HEADS, HEAD_DIM), cos, sin).transpose(0, 2, 1, 3)
    k = partial_rope(k.reshape(B, S, N_HEADS, HEAD_DIM), cos, sin).transpose(0, 2, 1, 3)
    v = v.reshape(B, S, N_HEADS, HEAD_DIM).transpose(0, 2, 1, 3)
    attn = moba_attention(q, k, v).transpose(0, 2, 1, 3).reshape(B, S, ATTN_WIDTH)
    y_attn = attn @ w_attn_br
    y_conv = conformer_conv(glu_in, dw_w, dw_b, cln_g, cln_b, w_conv_pw)
    merged = jax.nn.sigmoid(g_attn) * y_attn + jax.nn.sigmoid(g_conv) * y_conv
    return merged @ w_out


def setup_inputs(seed: int = 0) -> dict:
    key = jax.random.key(seed)
    ks = jax.random.split(key, 20)
    f32 = jnp.float32
    L, D, A, C = DEPTH, D_MODEL, ATTN_WIDTH, CONV_WIDTH
    nrm = lambda k, shape, s: jax.random.normal(k, shape, f32) * s
    return {
        "x": jax.random.normal(ks[0], (BATCH, SEQ, D), f32),
        "c": jax.random.normal(ks[1], (BATCH, D), f32),
        "positions": jnp.broadcast_to(jnp.arange(SEQ, dtype=jnp.int32), (BATCH, SEQ)),
        "w_in": nrm(ks[2], (L, D, IN_COLS), D ** -0.5),
        "w_attn_br": nrm(ks[3], (L, A, D), A ** -0.5),
        "conv_dw_w": nrm(ks[4], (L, CONV_KERNEL, C), CONV_KERNEL ** -0.5),
        "conv_dw_b": nrm(ks[5], (L, C), 0.02),
        "conv_ln_g": 1.0 + nrm(ks[6], (L, C), 0.02),
        "conv_ln_b": nrm(ks[7], (L, C), 0.02),
        "w_conv_pw": nrm(ks[8], (L, C, D), C ** -0.5),
        "w_out": nrm(ks[9], (L, D, D), DN_BETA * D ** -0.5),
        "w_ada": nrm(ks[10], (L, D, 6 * D), ADA_SCALE * D ** -0.5),
        "b_ada": nrm(ks[11], (L, 6 * D), 0.02),
        "ln_mix_g": 1.0 + nrm(ks[12], (L, D), 0.02),
        "ln_mix_b": nrm(ks[13], (L, D), 0.02),
        "w_up": nrm(ks[14], (L, D, D_FF), D ** -0.5),
        "w_down": nrm(ks[15], (L, D_FF, D), DN_BETA * D_FF ** -0.5),
        "ln_ffn_g": 1.0 + nrm(ks[16], (L, D), 0.02),
        "ln_ffn_b": nrm(ks[17], (L, D), 0.02),
    }


def reference(x, c, positions, w_in, w_attn_br, conv_dw_w, conv_dw_b, conv_ln_g, conv_ln_b,
              w_conv_pw, w_out, w_ada, b_ada, ln_mix_g, ln_mix_b, w_up, w_down,
              ln_ffn_g, ln_ffn_b):
    inv_freq = ROPE_THETA ** (-jnp.arange(0, ROT_DIM, 2, dtype=jnp.float32) / ROT_DIM)
    ang = positions.astype(jnp.float32)[..., None] * inv_freq
    cos = jnp.cos(ang)[:, :, None, :].astype(x.dtype)
    sin = jnp.sin(ang)[:, :, None, :].astype(x.dtype)
    c_act = jax.nn.silu(c)
    for l in range(DEPTH):
        ada = c_act @ w_ada[l] + b_ada[l]
        sh1, sc1, gt1, sh2, sc2, gt2 = [t[:, None, :] for t in jnp.split(ada, 6, axis=-1)]
        u = x * (1.0 + sc1) + sh1
        y = mixer(u, cos, sin, w_in[l], w_attn_br[l], conv_dw_w[l], conv_dw_b[l],
                  conv_ln_g[l], conv_ln_b[l], w_conv_pw[l], w_out[l])
        x = layer_norm(DN_ALPHA * x + (1.0 + gt1) * y, ln_mix_g[l], ln_mix_b[l])
        u = x * (1.0 + sc2) + sh2
        h = jnp.square(jax.nn.relu(u @ w_up[l])) @ w_down[l]
        x = layer_norm(DN_ALPHA * x + (1.0 + gt2) * h, ln_ffn_g[l], ln_ffn_b[l])
    return x
```

```python
import functools

import jax
import jax.numpy as jnp
from jax import lax
from jax.experimental import pallas as pl
from jax.experimental.pallas import tpu as pltpu

D_MODEL = 1024
N_HEADS = 8
HEAD_DIM = 64
ATTN_WIDTH = N_HEADS * HEAD_DIM
ROT_DIM = HEAD_DIM // 4
ROPE_THETA = 500000.0
MOBA_BLOCK = 256
MOBA_TOPK = 3
CONV_WIDTH = D_MODEL // 2
CONV_KERNEL = 31
D_FF = 4 * D_MODEL
LN_EPS = 1e-5
NEG_INF = -1e30

LANES = 128
HEADS_PER_GROUP = LANES // HEAD_DIM
N_GROUPS = N_HEADS // HEADS_PER_GROUP
CONV_HALO = 32
CONV_ROWS = 32
ROW_TILE = 512
VMEM_LIMIT = 56 * 1024 * 1024

F32 = jnp.float32
BF16 = jnp.bfloat16


def _dot(a, b):
    return jnp.dot(a, b, preferred_element_type=F32)


def _dot_nt(a, b):
    return lax.dot_general(a, b, (((1,), (1,)), ((), ())), preferred_element_type=F32)


def _dot_tn(a, b):
    return lax.dot_general(a, b, (((0,), (0,)), ((), ())), preferred_element_type=F32)


def _layer_norm(r, g, b):
    mu = jnp.mean(r, axis=-1, keepdims=True)
    d = r - mu
    var = jnp.mean(d * d, axis=-1, keepdims=True)
    return d * lax.rsqrt(var + LN_EPS) * g + b


def _params(*semantics):
    return pltpu.CompilerParams(dimension_semantics=semantics, vmem_limit_bytes=VMEM_LIMIT)


def _ada_kernel(c_ref, w_ref, b_ref, o_ref):
    c = c_ref[...]
    c_act = c * jax.nn.sigmoid(c)
    o_ref[0] = _dot(c_act.astype(BF16), w_ref[0].astype(BF16)) + b_ref[0]


def _ada_call(c_pad, w_ada, b_ada):
    depth, d, six_d = w_ada.shape
    rows = c_pad.shape[0]
    n_chunks = six_d // d
    return pl.pallas_call(
        _ada_kernel,
        grid=(depth, n_chunks),
        in_specs=[
            pl.BlockSpec((rows, d), lambda l, j: (0, 0)),
            pl.BlockSpec((1, d, d), lambda l, j: (l, 0, j)),
            pl.BlockSpec((1, 1, d), lambda l, j: (l, 0, j)),
        ],
        out_specs=pl.BlockSpec((1, rows, d), lambda l, j: (l, 0, j)),
        out_shape=jax.ShapeDtypeStruct((depth, rows, six_d), F32),
        compiler_params=_params("parallel", "parallel"),
        name="ada",
    )(c_pad, w_ada, b_ada.reshape(depth, 1, six_d))


def _inproj_kernel(x_ref, sc_ref, sh_ref, w_ref, cos_ref, sin_ref, dww_ref, dwb_ref, lng_ref, lnb_ref,
                   q_ref, k_ref, vt_ref, km_ref, ch_ref, hext_ref):
    tm = x_ref.shape[1]
    a_w = ATTN_WIDTH
    c_w = CONV_WIDTH

    @pl.when(pl.program_id(1) == 0)
    def _():
        hext_ref[0:CONV_HALO, :] = jnp.zeros((CONV_HALO, c_w), F32)

    u = (x_ref[0] * (1.0 + sc_ref[0]) + sh_ref[0]).astype(BF16)
    cosf = cos_ref[0]
    sinf = sin_ref[0]
    lane = lax.broadcasted_iota(jnp.int32, (1, LANES), 1)
    first_half = (lane % HEAD_DIM) < (ROT_DIM // 2)

    def rope(t, c):
        xc = t[:, c * LANES:(c + 1) * LANES]
        partner = jnp.where(first_half,
                            pltpu.roll(xc, LANES - ROT_DIM // 2, 1),
                            pltpu.roll(xc, ROT_DIM // 2, 1))
        return xc * cosf + partner * sinf

    q = _dot(u, w_ref[:, 0:a_w])
    for c in range(a_w // LANES):
        q_ref[0, :, c * LANES:(c + 1) * LANES] = rope(q, c).astype(BF16)

    k = _dot(u, w_ref[:, a_w:2 * a_w])
    for c in range(a_w // LANES):
        kc = rope(k, c)
        k_ref[0, :, c * LANES:(c + 1) * LANES] = kc.astype(BF16)
        for blk in range(tm // MOBA_BLOCK):
            km_ref[blk, :, c * LANES:(c + 1) * LANES] = jnp.mean(
                kc[blk * MOBA_BLOCK:(blk + 1) * MOBA_BLOCK], axis=0, keepdims=True)

    v = _dot(u, w_ref[:, 2 * a_w:3 * a_w])
    for blk in range(tm // MOBA_BLOCK):
        vt_ref[0, blk] = v[blk * MOBA_BLOCK:(blk + 1) * MOBA_BLOCK, :].T.astype(BF16)

    glu = _dot(u, w_ref[:, 3 * a_w:3 * a_w + 2 * c_w])
    hext_ref[CONV_HALO:CONV_HALO + tm, :] = glu[:, :c_w] * jax.nn.sigmoid(glu[:, c_w:])

    base = CONV_HALO - (CONV_KERNEL - 1)
    lng = lng_ref[...]
    lnb = lnb_ref[...]
    dwb = dwb_ref[...]
    for r in range(0, tm, CONV_ROWS):
        acc = jnp.broadcast_to(dwb, (CONV_ROWS, c_w))
        for j in range(CONV_KERNEL):
            acc = acc + dww_ref[j:j + 1, :] * hext_ref[base + r + j:base + r + j + CONV_ROWS, :]
        y = _layer_norm(acc, lng, lnb)
        ch_ref[0, r:r + CONV_ROWS, :] = (y * jax.nn.sigmoid(y)).astype(BF16)

    hext_ref[0:CONV_HALO, :] = hext_ref[tm:tm + CONV_HALO, :]


def _inproj_call(layer, x, ada, w_qkvg, cosf, sinf, dw_w, dw_b, cln_g, cln_b):
    b, s, d = x.shape
    tm = ROW_TILE
    n_s = s // tm
    blocks_per_tile = tm // MOBA_BLOCK
    n_blk = s // MOBA_BLOCK
    n_cols = w_qkvg.shape[2]
    c_w = CONV_WIDTH

    def ada_map(slot):
        return lambda bi, si: (_ada_row(layer, bi, slot), 0, 0)

    in_specs = [
        pl.BlockSpec((1, tm, d), lambda bi, si: (bi, si, 0)),
        pl.BlockSpec((1, 1, d), ada_map(1)),
        pl.BlockSpec((1, 1, d), ada_map(0)),
        pl.BlockSpec((None, d, n_cols), lambda bi, si: (layer, 0, 0)),
        pl.BlockSpec((1, tm, LANES), lambda bi, si: (bi, si, 0)),
        pl.BlockSpec((1, tm, LANES), lambda bi, si: (bi, si, 0)),
        pl.BlockSpec((None, CONV_KERNEL, c_w), lambda bi, si: (layer, 0, 0)),
        pl.BlockSpec((None, 1, c_w), lambda bi, si: (layer, 0, 0)),
        pl.BlockSpec((None, 1, c_w), lambda bi, si: (layer, 0, 0)),
        pl.BlockSpec((None, 1, c_w), lambda bi, si: (layer, 0, 0)),
    ]
    out_specs = [
        pl.BlockSpec((1, tm, ATTN_WIDTH), lambda bi, si: (bi, si, 0)),
        pl.BlockSpec((1, tm, ATTN_WIDTH), lambda bi, si: (bi, si, 0)),
        pl.BlockSpec((1, blocks_per_tile, ATTN_WIDTH, MOBA_BLOCK), lambda bi, si: (bi, si, 0, 0)),
        pl.BlockSpec((blocks_per_tile, 1, ATTN_WIDTH), lambda bi, si: (bi * n_s + si, 0, 0)),
        pl.BlockSpec((1, tm, c_w), lambda bi, si: (bi, si, 0)),
    ]
    out_shape = [
        jax.ShapeDtypeStruct((b, s, ATTN_WIDTH), BF16),
        jax.ShapeDtypeStruct((b, s, ATTN_WIDTH), BF16),
        jax.ShapeDtypeStruct((b, n_blk, ATTN_WIDTH, MOBA_BLOCK), BF16),
        jax.ShapeDtypeStruct((b * n_blk, 1, ATTN_WIDTH), F32),
        jax.ShapeDtypeStruct((b, s, c_w), BF16),
    ]
    return pl.pallas_call(
        _inproj_kernel,
        grid=(b, n_s),
        in_specs=in_specs,
        out_specs=out_specs,
        out_shape=out_shape,
        scratch_shapes=[pltpu.VMEM((tm + CONV_HALO, c_w), F32)],
        compiler_params=_params("parallel", "arbitrary"),
        name="inproj",
    )(x, ada, ada, w_qkvg, cosf, sinf, dw_w, dw_b, cln_g, cln_b)


ADA_SLOTS = 6
ADA_ROWS = 8


def _ada_row(layer, bi, slot):
    return (layer * ADA_ROWS + bi) * ADA_SLOTS + slot


def _attn_kernel(q_ref, k_ref, vt_ref, km_ref, o_ref, bias_ref):
    n_blk = k_ref.shape[1]
    tq = q_ref.shape[1]
    own = pl.program_id(2)
    q = q_ref[0]
    km = km_ref[0]
    lane = lax.broadcasted_iota(jnp.int32, (1, LANES), 1)
    blk_col = lax.broadcasted_iota(jnp.int32, (n_blk, 1), 0)
    valid = blk_col < own

    q_heads = []
    for hh in range(HEADS_PER_GROUP):
        in_head = (lane // HEAD_DIM) == hh
        q_h = jnp.where(in_head, q, jnp.zeros_like(q))
        km_h = jnp.where(in_head, km, 0.0)
        km_hi = km_h.astype(BF16)
        km_lo = (km_h - km_hi.astype(F32)).astype(BF16)
        gate = _dot_nt(km_hi, q_h) + _dot_nt(km_lo, q_h)
        gate = jnp.where(valid, gate, NEG_INF)
        rank = jnp.zeros((n_blk, tq), F32)
        for n2 in range(n_blk):
            row = gate[n2:n2 + 1, :]
            ahead = jnp.where(row > gate, 1.0, jnp.where(row == gate, (blk_col > n2).astype(F32), 0.0))
            rank = rank + ahead
        selected = jnp.where(rank < MOBA_TOPK, valid.astype(F32), 0.0)
        bias_ref[hh] = jnp.where(selected > 0.0, 0.0, -NEG_INF)
        q_heads.append(q_h * (HEAD_DIM ** -0.5))

    def update(state, s_t, bias, vt_h):
        m, l, acc = state
        m_blk = jnp.max(s_t, axis=0, keepdims=True)
        m_new = jnp.maximum(m, m_blk - bias)
        alpha = jnp.exp(m - m_new)
        p_t = jnp.exp(s_t - (m_new + bias))
        l_new = alpha * l + jnp.sum(p_t, axis=0, keepdims=True)
        acc_new = alpha * acc + _dot(vt_h, p_t.astype(BF16))
        return m_new, l_new, acc_new

    def past_block(n, states):
        k_blk = k_ref[0, n]
        vt_blk = vt_ref[0, n]
        out = []
        for hh in range(HEADS_PER_GROUP):
            s_t = _dot_nt(k_blk, q_heads[hh])
            bias = bias_ref[hh, pl.ds(n, 1), :]
            out.append(update(states[hh], s_t, bias, vt_blk[hh * HEAD_DIM:(hh + 1) * HEAD_DIM, :]))
        return tuple(out)

    init = tuple((jnp.full((1, tq), NEG_INF, F32), jnp.zeros((1, tq), F32), jnp.zeros((HEAD_DIM, tq), F32))
                 for _ in range(HEADS_PER_GROUP))
    states = lax.fori_loop(0, own, past_block, init)

    k_own = k_ref[0, own]
    vt_own = vt_ref[0, own]
    key_pos = lax.broadcasted_iota(jnp.int32, (MOBA_BLOCK, tq), 0)
    q_pos = lax.broadcasted_iota(jnp.int32, (MOBA_BLOCK, tq), 1)
    causal = key_pos <= q_pos
    zero_bias = jnp.zeros((1, tq), F32)
    for hh in range(HEADS_PER_GROUP):
        s_t = jnp.where(causal, _dot_nt(k_own, q_heads[hh]), NEG_INF)
        _, l, acc = update(states[hh], s_t, zero_bias, vt_own[hh * HEAD_DIM:(hh + 1) * HEAD_DIM, :])
        o_ref[0, hh * HEAD_DIM:(hh + 1) * HEAD_DIM, :] = (acc / l).astype(BF16)


def _attn_call(q, k4, vt, kmean):
    b, s, _ = q.shape
    n_blk = s // MOBA_BLOCK
    tq = MOBA_BLOCK
    return pl.pallas_call(
        _attn_kernel,
        grid=(b, N_GROUPS, n_blk),
        in_specs=[
            pl.BlockSpec((1, tq, LANES), lambda bi, g, i: (bi, i, g)),
            pl.BlockSpec((1, n_blk, MOBA_BLOCK, LANES), lambda bi, g, i: (bi, 0, 0, g)),
            pl.BlockSpec((1, n_blk, LANES, MOBA_BLOCK), lambda bi, g, i: (bi, 0, g, 0)),
            pl.BlockSpec((1, n_blk, LANES), lambda bi, g, i: (bi, 0, g)),
        ],
        out_specs=pl.BlockSpec((1, LANES, tq), lambda bi, g, i: (bi, g, i)),
        out_shape=jax.ShapeDtypeStruct((b, ATTN_WIDTH, s), BF16),
        scratch_shapes=[pltpu.VMEM((HEADS_PER_GROUP, n_blk, tq), F32)],
        compiler_params=_params("parallel", "parallel", "arbitrary"),
        name="moba_attn",
    )(q, k4, vt, kmean)


def _mix_kernel(alpha, x_ref, sc_ref, sh_ref, gt_ref, at_ref, ch_ref, wg_ref, wa_ref, wc_ref, wo_ref,
                lng_ref, lnb_ref, o_ref):
    d = D_MODEL
    x = x_ref[0]
    u = (x * (1.0 + sc_ref[0]) + sh_ref[0]).astype(BF16)
    y_attn = _dot_tn(at_ref[0], wa_ref[...])
    y_conv = _dot(ch_ref[0], wc_ref[...])
    merged = jax.nn.sigmoid(_dot(u, wg_ref[:, 0:d])) * y_attn
    merged = merged + jax.nn.sigmoid(_dot(u, wg_ref[:, d:2 * d])) * y_conv
    y = _dot(merged.astype(BF16), wo_ref[...])
    o_ref[0] = _layer_norm(alpha * x + (1.0 + gt_ref[0]) * y, lng_ref[...], lnb_ref[...])


def _mix_call(layer, alpha, x, ada, attn_t, conv_h, w_gate, w_attn_br, w_conv_pw, w_out, ln_g, ln_b):
    b, s, d = x.shape
    tm = ROW_TILE

    def ada_map(slot):
        return lambda bi, si: (_ada_row(layer, bi, slot), 0, 0)

    def whole(arr):
        return pl.BlockSpec((None,) + arr.shape[1:], lambda bi, si: (layer,) + (0,) * (arr.ndim - 1))

    return pl.pallas_call(
        functools.partial(_mix_kernel, alpha),
        grid=(b, s // tm),
        in_specs=[
            pl.BlockSpec((1, tm, d), lambda bi, si: (bi, si, 0)),
            pl.BlockSpec((1, 1, d), ada_map(1)),
            pl.BlockSpec((1, 1, d), ada_map(0)),
            pl.BlockSpec((1, 1, d), ada_map(2)),
            pl.BlockSpec((1, ATTN_WIDTH, tm), lambda bi, si: (bi, 0, si)),
            pl.BlockSpec((1, tm, CONV_WIDTH), lambda bi, si: (bi, si, 0)),
            whole(w_gate), whole(w_attn_br), whole(w_conv_pw), whole(w_out), whole(ln_g), whole(ln_b),
        ],
        out_specs=pl.BlockSpec((1, tm, d), lambda bi, si: (bi, si, 0)),
        out_shape=jax.ShapeDtypeStruct((b, s, d), F32),
        compiler_params=_params("parallel", "parallel"),
        name="mix_out",
    )(x, ada, ada, ada, attn_t, conv_h, w_gate, w_attn_br, w_conv_pw, w_out, ln_g, ln_b)


def _ffn_kernel(alpha, x_ref, sc_ref, sh_ref, gt_ref, wu_ref, wd_ref, lng_ref, lnb_ref, o_ref):
    d = D_MODEL
    x = x_ref[0]
    u = (x * (1.0 + sc_ref[0]) + sh_ref[0]).astype(BF16)
    y = None
    for c in range(D_FF // d):
        h = jnp.maximum(_dot(u, wu_ref[:, c * d:(c + 1) * d]), 0.0)
        part = _dot((h * h).astype(BF16), wd_ref[c * d:(c + 1) * d, :])
        y = part if y is None else y + part
    o_ref[0] = _layer_norm(alpha * x + (1.0 + gt_ref[0]) * y, lng_ref[...], lnb_ref[...])


def _ffn_call(layer, alpha, x, ada, w_up, w_down, ln_g, ln_b):
    b, s, d = x.shape
    tm = ROW_TILE

    def ada_map(slot):
        return lambda bi, si: (_ada_row(layer, bi, slot), 0, 0)

    def whole(arr):
        return pl.BlockSpec((None,) + arr.shape[1:], lambda bi, si: (layer,) + (0,) * (arr.ndim - 1),
                            pipeline_mode=pl.Buffered(1))

    return pl.pallas_call(
        functools.partial(_ffn_kernel, alpha),
        grid=(b, s // tm),
        in_specs=[
            pl.BlockSpec((1, tm, d), lambda bi, si: (bi, si, 0)),
            pl.BlockSpec((1, 1, d), ada_map(4)),
            pl.BlockSpec((1, 1, d), ada_map(3)),
            pl.BlockSpec((1, 1, d), ada_map(5)),
            whole(w_up), whole(w_down), whole(ln_g), whole(ln_b),
        ],
        out_specs=pl.BlockSpec((1, tm, d), lambda bi, si: (bi, si, 0)),
        out_shape=jax.ShapeDtypeStruct((b, s, d), F32),
        compiler_params=_params("parallel", "parallel"),
        name="ffn",
    )(x, ada, ada, ada, w_up, w_down, ln_g, ln_b)


def _rope_tables(positions):
    inv_freq = ROPE_THETA ** (-jnp.arange(0, ROT_DIM, 2, dtype=F32) / ROT_DIM)
    ang = positions.astype(F32)[..., None] * inv_freq
    cos, sin = jnp.cos(ang), jnp.sin(ang)
    rest = HEAD_DIM - ROT_DIM
    cos_h = jnp.concatenate([cos, cos, jnp.ones(cos.shape[:-1] + (rest,), F32)], -1)
    sin_h = jnp.concatenate([-sin, sin, jnp.zeros(sin.shape[:-1] + (rest,), F32)], -1)
    return jnp.tile(cos_h, (1, 1, HEADS_PER_GROUP)), jnp.tile(sin_h, (1, 1, HEADS_PER_GROUP))


def kernel(x, c, positions, w_in, w_attn_br, conv_dw_w, conv_dw_b, conv_ln_g, conv_ln_b, w_conv_pw, w_out,
           w_ada, b_ada, ln_mix_g, ln_mix_b, w_up, w_down, ln_ffn_g, ln_ffn_b):
    depth = w_in.shape[0]
    b, s, d = x.shape
    assert d == D_MODEL and s % ROW_TILE == 0 and ROW_TILE % MOBA_BLOCK == 0 and b <= ADA_ROWS
    alpha = (2 * depth) ** 0.25
    n_blk = s // MOBA_BLOCK
    qkvg_cols = 3 * ATTN_WIDTH + 2 * CONV_WIDTH

    c_pad = jnp.zeros((ADA_ROWS, d), F32).at[:b].set(c)
    ada = _ada_call(c_pad, w_ada, b_ada).reshape(depth * ADA_ROWS * ADA_SLOTS, 1, d)
    cosf, sinf = _rope_tables(positions)

    w_in_b = w_in.astype(BF16)
    w_qkvg = w_in_b[:, :, :qkvg_cols]
    w_gate = w_in_b[:, :, qkvg_cols:]
    w_attn_br_b = w_attn_br.astype(BF16)
    w_conv_pw_b = w_conv_pw.astype(BF16)
    w_out_b = w_out.astype(BF16)
    w_up_b = w_up.astype(BF16)
    w_down_b = w_down.astype(BF16)
    row = lambda t: t.reshape(depth, 1, t.shape[-1])

    for layer in range(depth):
        q, k, vt, kmean, conv_h = _inproj_call(layer, x, ada, w_qkvg, cosf, sinf, conv_dw_w,
                                               row(conv_dw_b), row(conv_ln_g), row(conv_ln_b))
        attn_t = _attn_call(q, k.reshape(b, n_blk, MOBA_BLOCK, ATTN_WIDTH), vt,
                            kmean.reshape(b, n_blk, ATTN_WIDTH))
        x = _mix_call(layer, alpha, x, ada, attn_t, conv_h, w_gate, w_attn_br_b, w_conv_pw_b, w_out_b,
                      row(ln_mix_g), row(ln_mix_b))
        x = _ffn_call(layer, alpha, x, ada, w_up_b, w_down_b, row(ln_ffn_g), row(ln_ffn_b))
    return x
```

```python
import functools

import jax
import jax.numpy as jnp
from jax import lax
from jax.experimental import pallas as pl
from jax.experimental.pallas import tpu as pltpu

D_MODEL = 1024
N_HEADS = 8
HEAD_DIM = 64
ATTN_WIDTH = N_HEADS * HEAD_DIM
ROT_DIM = HEAD_DIM // 4
ROPE_THETA = 500000.0
MOBA_BLOCK = 256
MOBA_TOPK = 3
CONV_WIDTH = D_MODEL // 2
CONV_KERNEL = 31
D_FF = 4 * D_MODEL
LN_EPS = 1e-5
NEG_INF = -1e30

LANES = 128
HEADS_PER_GROUP = LANES // HEAD_DIM
N_GROUPS = N_HEADS // HEADS_PER_GROUP
SUBLANES = 8
BF16_ROWS = 16
CONV_HALO = 32
CONV_ROWS = 32
VT_ROWS = HEAD_DIM + BF16_ROWS
SCORE_LEAD = 2
QK_SCALE = HEAD_DIM ** -0.5 * 1.4426950408889634
ROW_TILE = 512
VMEM_LIMIT = 56 * 1024 * 1024

F32 = jnp.float32
BF16 = jnp.bfloat16


def _dot(a, b):
    return jnp.dot(a, b, preferred_element_type=F32)


def _dot_nt(a, b):
    return lax.dot_general(a, b, (((1,), (1,)), ((), ())), preferred_element_type=F32)


def _dot_tn(a, b):
    return lax.dot_general(a, b, (((0,), (0,)), ((), ())), preferred_element_type=F32)


def _layer_norm(r, g, b):
    mu = jnp.mean(r, axis=-1, keepdims=True)
    d = r - mu
    var = jnp.mean(d * d, axis=-1, keepdims=True)
    return d * lax.rsqrt(var + LN_EPS) * g + b


def _params(*semantics):
    return pltpu.CompilerParams(dimension_semantics=semantics, vmem_limit_bytes=VMEM_LIMIT)


def _ada_kernel(c_ref, w_ref, b_ref, o_ref):
    c = c_ref[...]
    c_act = c * jax.nn.sigmoid(c)
    o_ref[0] = _dot(c_act.astype(BF16), w_ref[0].astype(BF16)) + b_ref[0]


def _ada_call(c_pad, w_ada, b_ada):
    depth, d, six_d = w_ada.shape
    rows = c_pad.shape[0]
    n_chunks = six_d // d
    return pl.pallas_call(
        _ada_kernel,
        grid=(depth, n_chunks),
        in_specs=[
            pl.BlockSpec((rows, d), lambda l, j: (0, 0)),
            pl.BlockSpec((1, d, d), lambda l, j: (l, 0, j)),
            pl.BlockSpec((1, 1, d), lambda l, j: (l, 0, j)),
        ],
        out_specs=pl.BlockSpec((1, rows, d), lambda l, j: (l, 0, j)),
        out_shape=jax.ShapeDtypeStruct((depth, rows, six_d), F32),
        compiler_params=_params("parallel", "parallel"),
        name="ada",
    )(c_pad, w_ada, b_ada.reshape(depth, 1, six_d))


def _inproj_kernel(x_ref, sc_ref, sh_ref, w_ref, cos_ref, sin_ref, dww_ref, dwb_ref, lng_ref, lnb_ref,
                   q_ref, k_ref, vt_ref, km_ref, ch_ref, hext_ref, hsh_ref):
    tm = x_ref.shape[1]
    a_w = ATTN_WIDTH
    c_w = CONV_WIDTH

    @pl.when(pl.program_id(1) == 0)
    def _():
        hext_ref[0:CONV_HALO, :] = jnp.zeros((CONV_HALO, c_w), F32)

    u = (x_ref[0] * (1.0 + sc_ref[0]) + sh_ref[0]).astype(BF16)
    cosf = cos_ref[0]
    sinf = sin_ref[0]
    lane = lax.broadcasted_iota(jnp.int32, (1, LANES), 1)
    first_half = (lane % HEAD_DIM) < (ROT_DIM // 2)

    def rope(t, c):
        xc = t[:, c * LANES:(c + 1) * LANES]
        partner = jnp.where(first_half,
                            pltpu.roll(xc, LANES - ROT_DIM // 2, 1),
                            pltpu.roll(xc, ROT_DIM // 2, 1))
        return xc * cosf + partner * sinf

    q = _dot(u, w_ref[:, 0:a_w])
    for c in range(a_w // LANES):
        q_ref[0, :, c * LANES:(c + 1) * LANES] = (rope(q, c) * QK_SCALE).astype(BF16)

    k = _dot(u, w_ref[:, a_w:2 * a_w])
    for c in range(a_w // LANES):
        kc = rope(k, c)
        k_ref[0, :, c * LANES:(c + 1) * LANES] = kc.astype(BF16)
        for blk in range(tm // MOBA_BLOCK):
            km_ref[blk, :, c * LANES:(c + 1) * LANES] = jnp.mean(
                kc[blk * MOBA_BLOCK:(blk + 1) * MOBA_BLOCK], axis=0, keepdims=True)

    v = _dot(u, w_ref[:, 2 * a_w:3 * a_w])
    ones_row = (lax.broadcasted_iota(jnp.int32, (BF16_ROWS, MOBA_BLOCK), 0) == 0).astype(BF16)
    for blk in range(tm // MOBA_BLOCK):
        v_t = v[blk * MOBA_BLOCK:(blk + 1) * MOBA_BLOCK, :].T.astype(BF16)
        for h in range(N_HEADS):
            vt_ref[0, blk, h * VT_ROWS:h * VT_ROWS + HEAD_DIM, :] = v_t[h * HEAD_DIM:(h + 1) * HEAD_DIM, :]
            vt_ref[0, blk, h * VT_ROWS + HEAD_DIM:(h + 1) * VT_ROWS, :] = ones_row

    glu = _dot(u, w_ref[:, 3 * a_w:3 * a_w + 2 * c_w])
    hext_ref[CONV_HALO:CONV_HALO + tm, :] = glu[:, :c_w] * jax.nn.sigmoid(glu[:, c_w:])

    base = CONV_HALO - (CONV_KERNEL - 1)
    sh_rows = hsh_ref.shape[1]
    for s in range(1, SUBLANES):
        hsh_ref[s - 1] = hext_ref[s:s + sh_rows, :]
    lng = lng_ref[...]
    lnb = lnb_ref[...]
    dwb = dwb_ref[...]
    for r in range(0, tm, CONV_ROWS):
        acc = jnp.broadcast_to(dwb, (CONV_ROWS, c_w))
        for j in range(CONV_KERNEL):
            s, row = (base + j) % SUBLANES, r + (base + j) // SUBLANES * SUBLANES
            tap = hext_ref[row:row + CONV_ROWS, :] if s == 0 else hsh_ref[s - 1, row:row + CONV_ROWS, :]
            acc = acc + dww_ref[j:j + 1, :] * tap
        y = _layer_norm(acc, lng, lnb)
        ch_ref[0, r:r + CONV_ROWS, :] = (y * jax.nn.sigmoid(y)).astype(BF16)

    hext_ref[0:CONV_HALO, :] = hext_ref[tm:tm + CONV_HALO, :]


def _inproj_call(layer, x, ada, w_qkvg, cosf, sinf, dw_w, dw_b, cln_g, cln_b):
    b, s, d = x.shape
    tm = ROW_TILE
    n_s = s // tm
    blocks_per_tile = tm // MOBA_BLOCK
    n_blk = s // MOBA_BLOCK
    n_cols = w_qkvg.shape[2]
    c_w = CONV_WIDTH

    def ada_map(slot):
        return lambda bi, si: (_ada_row(layer, bi, slot), 0, 0)

    in_specs = [
        pl.BlockSpec((1, tm, d), lambda bi, si: (bi, si, 0)),
        pl.BlockSpec((1, 1, d), ada_map(1)),
        pl.BlockSpec((1, 1, d), ada_map(0)),
        pl.BlockSpec((None, d, n_cols), lambda bi, si: (layer, 0, 0)),
        pl.BlockSpec((1, tm, LANES), lambda bi, si: (bi, si, 0)),
        pl.BlockSpec((1, tm, LANES), lambda bi, si: (bi, si, 0)),
        pl.BlockSpec((None, CONV_KERNEL, c_w), lambda bi, si: (layer, 0, 0)),
        pl.BlockSpec((None, 1, c_w), lambda bi, si: (layer, 0, 0)),
        pl.BlockSpec((None, 1, c_w), lambda bi, si: (layer, 0, 0)),
        pl.BlockSpec((None, 1, c_w), lambda bi, si: (layer, 0, 0)),
    ]
    out_specs = [
        pl.BlockSpec((1, tm, ATTN_WIDTH), lambda bi, si: (bi, si, 0)),
        pl.BlockSpec((1, tm, ATTN_WIDTH), lambda bi, si: (bi, si, 0)),
        pl.BlockSpec((1, blocks_per_tile, N_HEADS * VT_ROWS, MOBA_BLOCK), lambda bi, si: (bi, si, 0, 0)),
        pl.BlockSpec((blocks_per_tile, 1, ATTN_WIDTH), lambda bi, si: (bi * n_s + si, 0, 0)),
        pl.BlockSpec((1, tm, c_w), lambda bi, si: (bi, si, 0)),
    ]
    out_shape = [
        jax.ShapeDtypeStruct((b, s, ATTN_WIDTH), BF16),
        jax.ShapeDtypeStruct((b, s, ATTN_WIDTH), BF16),
        jax.ShapeDtypeStruct((b, n_blk, N_HEADS * VT_ROWS, MOBA_BLOCK), BF16),
        jax.ShapeDtypeStruct((b * n_blk, 1, ATTN_WIDTH), F32),
        jax.ShapeDtypeStruct((b, s, c_w), BF16),
    ]
    return pl.pallas_call(
        _inproj_kernel,
        grid=(b, n_s),
        in_specs=in_specs,
        out_specs=out_specs,
        out_shape=out_shape,
        scratch_shapes=[
            pltpu.VMEM((tm + CONV_HALO, c_w), F32),
            pltpu.VMEM((SUBLANES - 1, tm + CONV_HALO - SUBLANES, c_w), F32),
        ],
        compiler_params=_params("parallel", "arbitrary"),
        name="inproj",
    )(x, ada, ada, w_qkvg, cosf, sinf, dw_w, dw_b, cln_g, cln_b)


ADA_SLOTS = 6
ADA_ROWS = 8


def _ada_row(layer, bi, slot):
    return (layer * ADA_ROWS + bi) * ADA_SLOTS + slot


def _attn_kernel(q_ref, k_ref, vt_ref, km_ref, o_ref, bias_ref, qh_ref, s_ref, m_ref, acc_ref):
    n_blk = k_ref.shape[1]
    tq = q_ref.shape[1]
    own = pl.program_id(1)
    blk_col = lax.broadcasted_iota(jnp.int32, (n_blk, 1), 0)
    valid = blk_col < own
    lane = lax.broadcasted_iota(jnp.int32, (1, LANES), 1)

    def group_lanes(h):
        g = h // HEADS_PER_GROUP
        return slice(g * LANES, (g + 1) * LANES)

    def vt_rows(h):
        return slice(h * VT_ROWS, (h + 1) * VT_ROWS)

    def score_block(slot, h, block):
        n = jnp.minimum(block, own)
        s_ref[slot, h] = _dot_nt(k_ref[0, n, :, group_lanes(h)], qh_ref[h])

    for h in range(N_HEADS):
        in_head = (lane // HEAD_DIM) == (h % HEADS_PER_GROUP)
        q_g = q_ref[0, :, group_lanes(h)]
        q_h = jnp.where(in_head, q_g, jnp.zeros_like(q_g))
        qh_ref[h] = q_h
        km_h = jnp.where(in_head, km_ref[0, :, group_lanes(h)], 0.0)
        km_hi = km_h.astype(BF16)
        km_lo = (km_h - km_hi.astype(F32)).astype(BF16)
        gate = _dot_nt(km_hi, q_h) + _dot_nt(km_lo, q_h)
        gate = jnp.where(valid, gate, NEG_INF)
        rank = jnp.zeros((n_blk, tq), F32)
        for n2 in range(n_blk):
            row = gate[n2:n2 + 1, :]
            ahead = jnp.where(row > gate, 1.0, jnp.where(row == gate, (blk_col > n2).astype(F32), 0.0))
            rank = rank + ahead
        selected = jnp.where(rank < MOBA_TOPK, valid.astype(F32), 0.0)
        bias_ref[h] = jnp.where(selected > 0.0, 0.0, -NEG_INF)
        m_ref[h] = jnp.full((1, tq), NEG_INF, F32)
        acc_ref[h] = jnp.zeros((VT_ROWS, tq), F32)
    for h in range(N_HEADS):
        score_block(0, h, 0)

    def update(h, s_t, bias, vt_h):
        m = m_ref[h]
        m_new = jnp.maximum(m, jnp.max(s_t, axis=0, keepdims=True) - bias)
        alpha = jnp.exp2(m - m_new)
        p_t = jnp.exp2(s_t - (m_new + bias)).astype(BF16)
        m_ref[h] = m_new
        acc_ref[h] = alpha * acc_ref[h] + _dot(vt_h, p_t)

    def one_block(slot, block):
        for h in range(SCORE_LEAD):
            score_block(1 - slot, h, block + 1)
        for h in range(N_HEADS):
            update(h, s_ref[slot, h], bias_ref[h, pl.ds(block, 1), :], vt_ref[0, block, vt_rows(h), :])
            if h + SCORE_LEAD < N_HEADS:
                score_block(1 - slot, h + SCORE_LEAD, block + 1)

    def two_blocks(j, carry):
        one_block(0, 2 * j)
        one_block(1, jnp.minimum(2 * j + 1, own))
        return carry

    lax.fori_loop(0, (own + 1) // 2, two_blocks, 0)

    key_pos = lax.broadcasted_iota(jnp.int32, (MOBA_BLOCK, tq), 0)
    q_pos = lax.broadcasted_iota(jnp.int32, (MOBA_BLOCK, tq), 1)
    causal = key_pos <= q_pos
    zero_bias = jnp.zeros((1, tq), F32)
    for h in range(N_HEADS):
        update(h, jnp.where(causal, s_ref[0, h], NEG_INF), zero_bias, vt_ref[0, own, vt_rows(h), :])
        acc = acc_ref[h]
        o_ref[0, h * HEAD_DIM:(h + 1) * HEAD_DIM, :] = (acc[0:HEAD_DIM] / acc[HEAD_DIM:HEAD_DIM + 1]).astype(BF16)


def _attn_call(q, k4, vt, kmean):
    b, s, _ = q.shape
    n_blk = s // MOBA_BLOCK
    tq = MOBA_BLOCK
    return pl.pallas_call(
        _attn_kernel,
        grid=(b, n_blk),
        in_specs=[
            pl.BlockSpec((1, tq, ATTN_WIDTH), lambda bi, i: (bi, i, 0)),
            pl.BlockSpec((1, n_blk, MOBA_BLOCK, ATTN_WIDTH), lambda bi, i: (bi, 0, 0, 0)),
            pl.BlockSpec((1, n_blk, N_HEADS * VT_ROWS, MOBA_BLOCK), lambda bi, i: (bi, 0, 0, 0)),
            pl.BlockSpec((1, n_blk, ATTN_WIDTH), lambda bi, i: (bi, 0, 0)),
        ],
        out_specs=pl.BlockSpec((1, ATTN_WIDTH, tq), lambda bi, i: (bi, 0, i)),
        out_shape=jax.ShapeDtypeStruct((b, ATTN_WIDTH, s), BF16),
        scratch_shapes=[
            pltpu.VMEM((N_HEADS, n_blk, tq), F32),
            pltpu.VMEM((N_HEADS, tq, LANES), BF16),
            pltpu.VMEM((2, N_HEADS, MOBA_BLOCK, tq), F32),
            pltpu.VMEM((N_HEADS, 1, tq), F32),
            pltpu.VMEM((N_HEADS, VT_ROWS, tq), F32),
        ],
        compiler_params=_params("parallel", "arbitrary"),
        name="moba_attn",
    )(q, k4, vt, kmean)


def _mix_kernel(alpha, x_ref, sc_ref, sh_ref, gt_ref, at_ref, ch_ref, wg_ref, wa_ref, wc_ref, wo_ref,
                lng_ref, lnb_ref, o_ref):
    d = D_MODEL
    x = x_ref[0]
    u = (x * (1.0 + sc_ref[0]) + sh_ref[0]).astype(BF16)
    y_attn = _dot_tn(at_ref[0], wa_ref[...])
    y_conv = _dot(ch_ref[0], wc_ref[...])
    merged = jax.nn.sigmoid(_dot(u, wg_ref[:, 0:d])) * y_attn
    merged = merged + jax.nn.sigmoid(_dot(u, wg_ref[:, d:2 * d])) * y_conv
    y = _dot(merged.astype(BF16), wo_ref[...])
    o_ref[0] = _layer_norm(alpha * x + (1.0 + gt_ref[0]) * y, lng_ref[...], lnb_ref[...])


def _mix_call(layer, alpha, x, ada, attn_t, conv_h, w_gate, w_attn_br, w_conv_pw, w_out, ln_g, ln_b):
    b, s, d = x.shape
    tm = ROW_TILE

    def ada_map(slot):
        return lambda bi, si: (_ada_row(layer, bi, slot), 0, 0)

    def whole(arr):
        return pl.BlockSpec((None,) + arr.shape[1:], lambda bi, si: (layer,) + (0,) * (arr.ndim - 1))

    return pl.pallas_call(
        functools.partial(_mix_kernel, alpha),
        grid=(b, s // tm),
        in_specs=[
            pl.BlockSpec((1, tm, d), lambda bi, si: (bi, si, 0)),
            pl.BlockSpec((1, 1, d), ada_map(1)),
            pl.BlockSpec((1, 1, d), ada_map(0)),
            pl.BlockSpec((1, 1, d), ada_map(2)),
            pl.BlockSpec((1, ATTN_WIDTH, tm), lambda bi, si: (bi, 0, si)),
            pl.BlockSpec((1, tm, CONV_WIDTH), lambda bi, si: (bi, si, 0)),
            whole(w_gate), whole(w_attn_br), whole(w_conv_pw), whole(w_out), whole(ln_g), whole(ln_b),
        ],
        out_specs=pl.BlockSpec((1, tm, d), lambda bi, si: (bi, si, 0)),
        out_shape=jax.ShapeDtypeStruct((b, s, d), F32),
        compiler_params=_params("parallel", "parallel"),
        name="mix_out",
    )(x, ada, ada, ada, attn_t, conv_h, w_gate, w_attn_br, w_conv_pw, w_out, ln_g, ln_b)


def _ffn_kernel(alpha, x_ref, sc_ref, sh_ref, gt_ref, wu_ref, wd_ref, lng_ref, lnb_ref, o_ref):
    d = D_MODEL
    x = x_ref[0]
    u = (x * (1.0 + sc_ref[0]) + sh_ref[0]).astype(BF16)
    y = None
    for c in range(D_FF // d):
        h = jnp.maximum(_dot(u, wu_ref[:, c * d:(c + 1) * d]), 0.0)
        part = _dot((h * h).astype(BF16), wd_ref[c * d:(c + 1) * d, :])
        y = part if y is None else y + part
    o_ref[0] = _layer_norm(alpha * x + (1.0 + gt_ref[0]) * y, lng_ref[...], lnb_ref[...])


def _ffn_call(layer, alpha, x, ada, w_up, w_down, ln_g, ln_b):
    b, s, d = x.shape
    tm = ROW_TILE

    def ada_map(slot):
        return lambda bi, si: (_ada_row(layer, bi, slot), 0, 0)

    def whole(arr):
        return pl.BlockSpec((None,) + arr.shape[1:], lambda bi, si: (layer,) + (0,) * (arr.ndim - 1),
                            pipeline_mode=pl.Buffered(1))

    return pl.pallas_call(
        functools.partial(_ffn_kernel, alpha),
        grid=(b, s // tm),
        in_specs=[
            pl.BlockSpec((1, tm, d), lambda bi, si: (bi, si, 0)),
            pl.BlockSpec((1, 1, d), ada_map(4)),
            pl.BlockSpec((1, 1, d), ada_map(3)),
            pl.BlockSpec((1, 1, d), ada_map(5)),
            whole(w_up), whole(w_down), whole(ln_g), whole(ln_b),
        ],
        out_specs=pl.BlockSpec((1, tm, d), lambda bi, si: (bi, si, 0)),
        out_shape=jax.ShapeDtypeStruct((b, s, d), F32),
        compiler_params=_params("parallel", "parallel"),
        name="ffn",
    )(x, ada, ada, ada, w_up, w_down, ln_g, ln_b)


def _rope_tables(positions):
    inv_freq = ROPE_THETA ** (-jnp.arange(0, ROT_DIM, 2, dtype=F32) / ROT_DIM)
    ang = positions.astype(F32)[..., None] * inv_freq
    cos, sin = jnp.cos(ang), jnp.sin(ang)
    rest = HEAD_DIM - ROT_DIM
    cos_h = jnp.concatenate([cos, cos, jnp.ones(cos.shape[:-1] + (rest,), F32)], -1)
    sin_h = jnp.concatenate([-sin, sin, jnp.zeros(sin.shape[:-1] + (rest,), F32)], -1)
    return jnp.tile(cos_h, (1, 1, HEADS_PER_GROUP)), jnp.tile(sin_h, (1, 1, HEADS_PER_GROUP))


def kernel(x, c, positions, w_in, w_attn_br, conv_dw_w, conv_dw_b, conv_ln_g, conv_ln_b, w_conv_pw, w_out,
           w_ada, b_ada, ln_mix_g, ln_mix_b, w_up, w_down, ln_ffn_g, ln_ffn_b):
    depth = w_in.shape[0]
    b, s, d = x.shape
    assert d == D_MODEL and s % ROW_TILE == 0 and ROW_TILE % MOBA_BLOCK == 0 and b <= ADA_ROWS
    alpha = (2 * depth) ** 0.25
    n_blk = s // MOBA_BLOCK
    qkvg_cols = 3 * ATTN_WIDTH + 2 * CONV_WIDTH

    c_pad = jnp.zeros((ADA_ROWS, d), F32).at[:b].set(c)
    ada = _ada_call(c_pad, w_ada, b_ada).reshape(depth * ADA_ROWS * ADA_SLOTS, 1, d)
    cosf, sinf = _rope_tables(positions)

    w_in_b = w_in.astype(BF16)
    w_qkvg = w_in_b[:, :, :qkvg_cols]
    w_gate = w_in_b[:, :, qkvg_cols:]
    w_attn_br_b = w_attn_br.astype(BF16)
    w_conv_pw_b = w_conv_pw.astype(BF16)
    w_out_b = w_out.astype(BF16)
    w_up_b = w_up.astype(BF16)
    w_down_b = w_down.astype(BF16)
    row = lambda t: t.reshape(depth, 1, t.shape[-1])

    for layer in range(depth):
        q, k, vt, kmean, conv_h = _inproj_call(layer, x, ada, w_qkvg, cosf, sinf, conv_dw_w,
                                               row(conv_dw_b), row(conv_ln_g), row(conv_ln_b))
        attn_t = _attn_call(q, k.reshape(b, n_blk, MOBA_BLOCK, ATTN_WIDTH), vt,
                            kmean.reshape(b, n_blk, ATTN_WIDTH))
        x = _mix_call(layer, alpha, x, ada, attn_t, conv_h, w_gate, w_attn_br_b, w_conv_pw_b, w_out_b,
                      row(ln_mix_g), row(ln_mix_b))
        x = _ffn_call(layer, alpha, x, ada, w_up_b, w_down_b, row(ln_ffn_g), row(ln_ffn_b))
    return x
```

```python
import functools

import jax
import jax.numpy as jnp
from jax import lax
from jax.experimental import pallas as pl
from jax.experimental.pallas import tpu as pltpu

D_MODEL = 1024
N_HEADS = 8
HEAD_DIM = 64
ATTN_WIDTH = N_HEADS * HEAD_DIM
ROT_DIM = HEAD_DIM // 4
ROPE_THETA = 500000.0
MOBA_BLOCK = 256
MOBA_TOPK = 3
CONV_WIDTH = D_MODEL // 2
CONV_KERNEL = 31
D_FF = 4 * D_MODEL
LN_EPS = 1e-5
NEG_INF = -1e30

LANES = 128
HEADS_PER_GROUP = LANES // HEAD_DIM
SUBLANES = 8
BF16_ROWS = 16
CONV_HALO = 32
CONV_ROWS = 32
VT_ROWS = HEAD_DIM + BF16_ROWS
SCORE_LEAD = 3
QK_SCALE = HEAD_DIM ** -0.5 * 1.4426950408889634
ROW_TILE = 512
VMEM_LIMIT = 56 * 1024 * 1024

F32 = jnp.float32
BF16 = jnp.bfloat16


def _dot(a, b):
    return jnp.dot(a, b, preferred_element_type=F32)


def _dot_nt(a, b):
    return lax.dot_general(a, b, (((1,), (1,)), ((), ())), preferred_element_type=F32)


def _dot_tn(a, b):
    return lax.dot_general(a, b, (((0,), (0,)), ((), ())), preferred_element_type=F32)


def _layer_norm(r, g, b):
    mu = jnp.mean(r, axis=-1, keepdims=True)
    d = r - mu
    var = jnp.mean(d * d, axis=-1, keepdims=True)
    return d * lax.rsqrt(var + LN_EPS) * g + b


def _params(*semantics):
    return pltpu.CompilerParams(dimension_semantics=semantics, vmem_limit_bytes=VMEM_LIMIT)


def _ada_kernel(c_ref, w_ref, b_ref, o_ref):
    c = c_ref[...]
    c_act = c * jax.nn.sigmoid(c)
    o_ref[0] = _dot(c_act.astype(BF16), w_ref[0].astype(BF16)) + b_ref[0]


def _ada_call(c_pad, w_ada, b_ada):
    depth, d, six_d = w_ada.shape
    rows = c_pad.shape[0]
    n_chunks = six_d // d
    return pl.pallas_call(
        _ada_kernel,
        grid=(depth, n_chunks),
        in_specs=[
            pl.BlockSpec((rows, d), lambda l, j: (0, 0)),
            pl.BlockSpec((1, d, d), lambda l, j: (l, 0, j)),
            pl.BlockSpec((1, 1, d), lambda l, j: (l, 0, j)),
        ],
        out_specs=pl.BlockSpec((1, rows, d), lambda l, j: (l, 0, j)),
        out_shape=jax.ShapeDtypeStruct((depth, rows, six_d), F32),
        compiler_params=_params("parallel", "parallel"),
        name="ada",
    )(c_pad, w_ada, b_ada.reshape(depth, 1, six_d))


def _inproj_kernel(x_ref, sc_ref, sh_ref, w_ref, cos_ref, sin_ref, dww_ref, dwb_ref, lng_ref, lnb_ref,
                   q_ref, k_ref, vt_ref, km_ref, ch_ref, hext_ref, hsh_ref):
    tm = x_ref.shape[1]
    a_w = ATTN_WIDTH
    c_w = CONV_WIDTH

    @pl.when(pl.program_id(1) == 0)
    def _():
        hext_ref[0:CONV_HALO, :] = jnp.zeros((CONV_HALO, c_w), F32)

    u = (x_ref[0] * (1.0 + sc_ref[0]) + sh_ref[0]).astype(BF16)
    cosf = cos_ref[0]
    sinf = sin_ref[0]
    lane = lax.broadcasted_iota(jnp.int32, (1, LANES), 1)
    first_half = (lane % HEAD_DIM) < (ROT_DIM // 2)

    def rope(t, c):
        xc = t[:, c * LANES:(c + 1) * LANES]
        partner = jnp.where(first_half,
                            pltpu.roll(xc, LANES - ROT_DIM // 2, 1),
                            pltpu.roll(xc, ROT_DIM // 2, 1))
        return xc * cosf + partner * sinf

    glu = _dot(u, w_ref[:, 3 * a_w:3 * a_w + 2 * c_w])
    hext_ref[CONV_HALO:CONV_HALO + tm, :] = glu[:, :c_w] * jax.nn.sigmoid(glu[:, c_w:])

    q = _dot(u, w_ref[:, 0:a_w])
    for c in range(a_w // LANES):
        q_ref[0, :, c * LANES:(c + 1) * LANES] = (rope(q, c) * QK_SCALE).astype(BF16)

    k = _dot(u, w_ref[:, a_w:2 * a_w])
    for c in range(a_w // LANES):
        kc = rope(k, c)
        k_ref[0, :, c * LANES:(c + 1) * LANES] = kc.astype(BF16)
        for blk in range(tm // MOBA_BLOCK):
            km_ref[blk, :, c * LANES:(c + 1) * LANES] = jnp.mean(
                kc[blk * MOBA_BLOCK:(blk + 1) * MOBA_BLOCK], axis=0, keepdims=True)

    v = _dot(u, w_ref[:, 2 * a_w:3 * a_w])
    ones_row = (lax.broadcasted_iota(jnp.int32, (BF16_ROWS, MOBA_BLOCK), 0) == 0).astype(BF16)
    for blk in range(tm // MOBA_BLOCK):
        v_t = v[blk * MOBA_BLOCK:(blk + 1) * MOBA_BLOCK, :].T.astype(BF16)
        for h in range(N_HEADS):
            vt_ref[0, blk, h * VT_ROWS:h * VT_ROWS + HEAD_DIM, :] = v_t[h * HEAD_DIM:(h + 1) * HEAD_DIM, :]
            vt_ref[0, blk, h * VT_ROWS + HEAD_DIM:(h + 1) * VT_ROWS, :] = ones_row

    base = CONV_HALO - (CONV_KERNEL - 1)
    sh_rows = hsh_ref.shape[1]
    for s in range(1, SUBLANES):
        hsh_ref[s - 1] = hext_ref[s:s + sh_rows, :]
    lng = lng_ref[...]
    lnb = lnb_ref[...]
    dwb = dwb_ref[...]
    groups = CONV_ROWS // SUBLANES
    for r in range(0, tm, CONV_ROWS):
        acc = jnp.broadcast_to(dwb.reshape(1, 1, c_w), (groups, SUBLANES, c_w))
        for j in range(CONV_KERNEL):
            s, row = (base + j) % SUBLANES, r + (base + j) // SUBLANES * SUBLANES
            tap = hext_ref[row:row + CONV_ROWS, :] if s == 0 else hsh_ref[s - 1, row:row + CONV_ROWS, :]
            acc = acc + dww_ref[j][None] * tap.reshape(groups, SUBLANES, c_w)
        y = _layer_norm(acc.reshape(CONV_ROWS, c_w), lng, lnb)
        ch_ref[0, r:r + CONV_ROWS, :] = (y * jax.nn.sigmoid(y)).astype(BF16)

    hext_ref[0:CONV_HALO, :] = hext_ref[tm:tm + CONV_HALO, :]


def _inproj_call(layer, x, ada, w_qkvg, cosf, sinf, dw_w, dw_b, cln_g, cln_b):
    b, s, d = x.shape
    tm = ROW_TILE
    n_s = s // tm
    blocks_per_tile = tm // MOBA_BLOCK
    n_blk = s // MOBA_BLOCK
    n_cols = w_qkvg.shape[2]
    c_w = CONV_WIDTH

    def ada_map(slot):
        return lambda bi, si: (_ada_row(layer, bi, slot), 0, 0)

    in_specs = [
        pl.BlockSpec((1, tm, d), lambda bi, si: (bi, si, 0)),
        pl.BlockSpec((1, 1, d), ada_map(1)),
        pl.BlockSpec((1, 1, d), ada_map(0)),
        pl.BlockSpec((None, d, n_cols), lambda bi, si: (layer, 0, 0)),
        pl.BlockSpec((1, tm, LANES), lambda bi, si: (bi, si, 0)),
        pl.BlockSpec((1, tm, LANES), lambda bi, si: (bi, si, 0)),
        pl.BlockSpec((None, CONV_KERNEL, SUBLANES, c_w), lambda bi, si: (layer, 0, 0, 0)),
        pl.BlockSpec((None, 1, c_w), lambda bi, si: (layer, 0, 0)),
        pl.BlockSpec((None, 1, c_w), lambda bi, si: (layer, 0, 0)),
        pl.BlockSpec((None, 1, c_w), lambda bi, si: (layer, 0, 0)),
    ]
    out_specs = [
        pl.BlockSpec((1, tm, ATTN_WIDTH), lambda bi, si: (bi, si, 0)),
        pl.BlockSpec((1, tm, ATTN_WIDTH), lambda bi, si: (bi, si, 0)),
        pl.BlockSpec((1, blocks_per_tile, N_HEADS * VT_ROWS, MOBA_BLOCK), lambda bi, si: (bi, si, 0, 0)),
        pl.BlockSpec((blocks_per_tile, 1, ATTN_WIDTH), lambda bi, si: (bi * n_s + si, 0, 0)),
        pl.BlockSpec((1, tm, c_w), lambda bi, si: (bi, si, 0)),
    ]
    out_shape = [
        jax.ShapeDtypeStruct((b, s, ATTN_WIDTH), BF16),
        jax.ShapeDtypeStruct((b, s, ATTN_WIDTH), BF16),
        jax.ShapeDtypeStruct((b, n_blk, N_HEADS * VT_ROWS, MOBA_BLOCK), BF16),
        jax.ShapeDtypeStruct((b * n_blk, 1, ATTN_WIDTH), F32),
        jax.ShapeDtypeStruct((b, s, c_w), BF16),
    ]
    return pl.pallas_call(
        _inproj_kernel,
        grid=(b, n_s),
        in_specs=in_specs,
        out_specs=out_specs,
        out_shape=out_shape,
        scratch_shapes=[
            pltpu.VMEM((tm + CONV_HALO, c_w), F32),
            pltpu.VMEM((SUBLANES - 1, tm + CONV_HALO - SUBLANES, c_w), F32),
        ],
        compiler_params=_params("parallel", "arbitrary"),
        name="inproj",
    )(x, ada, ada, w_qkvg, cosf, sinf, dw_w, dw_b, cln_g, cln_b)


ADA_SLOTS = 6
ADA_ROWS = 8


def _ada_row(layer, bi, slot):
    return (layer * ADA_ROWS + bi) * ADA_SLOTS + slot


def _attn_kernel(q_ref, k_ref, vt_ref, km_ref, o_ref, bias_ref, qh_ref, s_ref, smax_ref, m_ref, acc_ref):
    n_blk = k_ref.shape[1]
    tq = q_ref.shape[1]
    own = pl.program_id(1)
    blk_col = lax.broadcasted_iota(jnp.int32, (n_blk, 1), 0)
    valid = blk_col < own
    lane = lax.broadcasted_iota(jnp.int32, (1, LANES), 1)

    def group_lanes(h):
        g = h // HEADS_PER_GROUP
        return slice(g * LANES, (g + 1) * LANES)

    def vt_rows(h):
        return slice(h * VT_ROWS, (h + 1) * VT_ROWS)

    def score_block(slot, h, block):
        n = jnp.minimum(block, own)
        s_t = _dot_nt(k_ref[0, n, :, group_lanes(h)], qh_ref[h])
        s_ref[slot, h] = s_t
        smax_ref[slot, h] = jnp.max(s_t, axis=0, keepdims=True)

    for h in range(N_HEADS):
        in_head = (lane // HEAD_DIM) == (h % HEADS_PER_GROUP)
        q_g = q_ref[0, :, group_lanes(h)]
        q_h = jnp.where(in_head, q_g, jnp.zeros_like(q_g))
        qh_ref[h] = q_h
        km_h = jnp.where(in_head, km_ref[0, :, group_lanes(h)], 0.0)
        km_hi = km_h.astype(BF16)
        km_lo = (km_h - km_hi.astype(F32)).astype(BF16)
        gate = _dot_nt(km_hi, q_h) + _dot_nt(km_lo, q_h)
        gate = jnp.where(valid, gate, NEG_INF)
        rank = jnp.zeros((n_blk, tq), F32)
        for n2 in range(n_blk):
            row = gate[n2:n2 + 1, :]
            ahead = jnp.where(row > gate, 1.0, jnp.where(row == gate, (blk_col > n2).astype(F32), 0.0))
            rank = rank + ahead
        selected = jnp.where(rank < MOBA_TOPK, valid.astype(F32), 0.0)
        bias_ref[h] = jnp.where(selected > 0.0, 0.0, -NEG_INF)
        m_ref[h] = jnp.full((1, tq), NEG_INF, F32)
        acc_ref[h] = jnp.zeros((VT_ROWS, tq), F32)
    for h in range(N_HEADS):
        score_block(0, h, 0)

    def update(h, s_t, s_max, bias, vt_h):
        m = m_ref[h]
        m_new = jnp.maximum(m, s_max - bias)
        alpha = jnp.exp2(m - m_new)
        p_t = jnp.exp2(s_t - (m_new + bias)).astype(BF16)
        m_ref[h] = m_new
        acc_ref[h] = alpha * acc_ref[h] + _dot(vt_h, p_t)

    def one_block(slot, block):
        for h in range(SCORE_LEAD):
            score_block(1 - slot, h, block + 1)
        for h in range(N_HEADS):
            update(h, s_ref[slot, h], smax_ref[slot, h], bias_ref[h, pl.ds(block, 1), :],
                   vt_ref[0, block, vt_rows(h), :])
            if h + SCORE_LEAD < N_HEADS:
                score_block(1 - slot, h + SCORE_LEAD, block + 1)

    def two_blocks(j, carry):
        one_block(0, 2 * j)
        one_block(1, jnp.minimum(2 * j + 1, own))
        return carry

    lax.fori_loop(0, (own + 1) // 2, two_blocks, 0)

    key_pos = lax.broadcasted_iota(jnp.int32, (MOBA_BLOCK, tq), 0)
    q_pos = lax.broadcasted_iota(jnp.int32, (MOBA_BLOCK, tq), 1)
    causal = key_pos <= q_pos
    zero_bias = jnp.zeros((1, tq), F32)
    for h in range(N_HEADS):
        s_own = jnp.where(causal, s_ref[0, h], NEG_INF)
        update(h, s_own, jnp.max(s_own, axis=0, keepdims=True), zero_bias, vt_ref[0, own, vt_rows(h), :])
        acc = acc_ref[h]
        o_ref[0, h * HEAD_DIM:(h + 1) * HEAD_DIM, :] = (acc[0:HEAD_DIM] / acc[HEAD_DIM:HEAD_DIM + 1]).astype(BF16)


def _attn_call(q, k4, vt, kmean):
    b, s, _ = q.shape
    n_blk = s // MOBA_BLOCK
    tq = MOBA_BLOCK
    return pl.pallas_call(
        _attn_kernel,
        grid=(b, n_blk),
        in_specs=[
            pl.BlockSpec((1, tq, ATTN_WIDTH), lambda bi, i: (bi, i, 0)),
            pl.BlockSpec((1, n_blk, MOBA_BLOCK, ATTN_WIDTH), lambda bi, i: (bi, 0, 0, 0)),
            pl.BlockSpec((1, n_blk, N_HEADS * VT_ROWS, MOBA_BLOCK), lambda bi, i: (bi, 0, 0, 0)),
            pl.BlockSpec((1, n_blk, ATTN_WIDTH), lambda bi, i: (bi, 0, 0)),
        ],
        out_specs=pl.BlockSpec((1, ATTN_WIDTH, tq), lambda bi, i: (bi, 0, i)),
        out_shape=jax.ShapeDtypeStruct((b, ATTN_WIDTH, s), BF16),
        scratch_shapes=[
            pltpu.VMEM((N_HEADS, n_blk, tq), F32),
            pltpu.VMEM((N_HEADS, tq, LANES), BF16),
            pltpu.VMEM((2, N_HEADS, MOBA_BLOCK, tq), F32),
            pltpu.VMEM((2, N_HEADS, 1, tq), F32),
            pltpu.VMEM((N_HEADS, 1, tq), F32),
            pltpu.VMEM((N_HEADS, VT_ROWS, tq), F32),
        ],
        compiler_params=_params("parallel", "arbitrary"),
        name="moba_attn",
    )(q, k4, vt, kmean)


def _mix_kernel(alpha, x_ref, sc_ref, sh_ref, gt_ref, at_ref, ch_ref, wg_ref, wa_ref, wc_ref, wo_ref,
                lng_ref, lnb_ref, o_ref):
    d = D_MODEL
    x = x_ref[0]
    u = (x * (1.0 + sc_ref[0]) + sh_ref[0]).astype(BF16)
    y_attn = _dot_tn(at_ref[0], wa_ref[...])
    y_conv = _dot(ch_ref[0], wc_ref[...])
    merged = jax.nn.sigmoid(_dot(u, wg_ref[:, 0:d])) * y_attn
    merged = merged + jax.nn.sigmoid(_dot(u, wg_ref[:, d:2 * d])) * y_conv
    y = _dot(merged.astype(BF16), wo_ref[...])
    o_ref[0] = _layer_norm(alpha * x + (1.0 + gt_ref[0]) * y, lng_ref[...], lnb_ref[...])


def _mix_call(layer, alpha, x, ada, attn_t, conv_h, w_gate, w_attn_br, w_conv_pw, w_out, ln_g, ln_b):
    b, s, d = x.shape
    tm = ROW_TILE

    def ada_map(slot):
        return lambda bi, si: (_ada_row(layer, bi, slot), 0, 0)

    def whole(arr):
        return pl.BlockSpec((None,) + arr.shape[1:], lambda bi, si: (layer,) + (0,) * (arr.ndim - 1))

    return pl.pallas_call(
        functools.partial(_mix_kernel, alpha),
        grid=(b, s // tm),
        in_specs=[
            pl.BlockSpec((1, tm, d), lambda bi, si: (bi, si, 0)),
            pl.BlockSpec((1, 1, d), ada_map(1)),
            pl.BlockSpec((1, 1, d), ada_map(0)),
            pl.BlockSpec((1, 1, d), ada_map(2)),
            pl.BlockSpec((1, ATTN_WIDTH, tm), lambda bi, si: (bi, 0, si)),
            pl.BlockSpec((1, tm, CONV_WIDTH), lambda bi, si: (bi, si, 0)),
            whole(w_gate), whole(w_attn_br), whole(w_conv_pw), whole(w_out), whole(ln_g), whole(ln_b),
        ],
        out_specs=pl.BlockSpec((1, tm, d), lambda bi, si: (bi, si, 0)),
        out_shape=jax.ShapeDtypeStruct((b, s, d), F32),
        compiler_params=_params("parallel", "parallel"),
        name="mix_out",
    )(x, ada, ada, ada, attn_t, conv_h, w_gate, w_attn_br, w_conv_pw, w_out, ln_g, ln_b)


def _ffn_kernel(alpha, x_ref, sc_ref, sh_ref, gt_ref, wu_ref, wd_ref, lng_ref, lnb_ref, o_ref):
    d = D_MODEL
    x = x_ref[0]
    u = (x * (1.0 + sc_ref[0]) + sh_ref[0]).astype(BF16)
    y = None
    for c in range(D_FF // d):
        h = jnp.maximum(_dot(u, wu_ref[:, c * d:(c + 1) * d]), 0.0)
        part = _dot((h * h).astype(BF16), wd_ref[c * d:(c + 1) * d, :])
        y = part if y is None else y + part
    o_ref[0] = _layer_norm(alpha * x + (1.0 + gt_ref[0]) * y, lng_ref[...], lnb_ref[...])


def _ffn_call(layer, alpha, x, ada, w_up, w_down, ln_g, ln_b):
    b, s, d = x.shape
    tm = ROW_TILE

    def ada_map(slot):
        return lambda bi, si: (_ada_row(layer, bi, slot), 0, 0)

    def whole(arr):
        return pl.BlockSpec((None,) + arr.shape[1:], lambda bi, si: (layer,) + (0,) * (arr.ndim - 1),
                            pipeline_mode=pl.Buffered(1))

    return pl.pallas_call(
        functools.partial(_ffn_kernel, alpha),
        grid=(b, s // tm),
        in_specs=[
            pl.BlockSpec((1, tm, d), lambda bi, si: (bi, si, 0)),
            pl.BlockSpec((1, 1, d), ada_map(4)),
            pl.BlockSpec((1, 1, d), ada_map(3)),
            pl.BlockSpec((1, 1, d), ada_map(5)),
            whole(w_up), whole(w_down), whole(ln_g), whole(ln_b),
        ],
        out_specs=pl.BlockSpec((1, tm, d), lambda bi, si: (bi, si, 0)),
        out_shape=jax.ShapeDtypeStruct((b, s, d), F32),
        compiler_params=_params("parallel", "parallel"),
        name="ffn",
    )(x, ada, ada, ada, w_up, w_down, ln_g, ln_b)


def _rope_tables(positions):
    inv_freq = ROPE_THETA ** (-jnp.arange(0, ROT_DIM, 2, dtype=F32) / ROT_DIM)
    ang = positions.astype(F32)[..., None] * inv_freq
    cos, sin = jnp.cos(ang), jnp.sin(ang)
    rest = HEAD_DIM - ROT_DIM
    cos_h = jnp.concatenate([cos, cos, jnp.ones(cos.shape[:-1] + (rest,), F32)], -1)
    sin_h = jnp.concatenate([-sin, sin, jnp.zeros(sin.shape[:-1] + (rest,), F32)], -1)
    return jnp.tile(cos_h, (1, 1, HEADS_PER_GROUP)), jnp.tile(sin_h, (1, 1, HEADS_PER_GROUP))


def kernel(x, c, positions, w_in, w_attn_br, conv_dw_w, conv_dw_b, conv_ln_g, conv_ln_b, w_conv_pw, w_out,
           w_ada, b_ada, ln_mix_g, ln_mix_b, w_up, w_down, ln_ffn_g, ln_ffn_b):
    depth = w_in.shape[0]
    b, s, d = x.shape
    assert d == D_MODEL and s % ROW_TILE == 0 and ROW_TILE % MOBA_BLOCK == 0 and b <= ADA_ROWS
    alpha = (2 * depth) ** 0.25
    n_blk = s // MOBA_BLOCK
    qkvg_cols = 3 * ATTN_WIDTH + 2 * CONV_WIDTH

    c_pad = jnp.zeros((ADA_ROWS, d), F32).at[:b].set(c)
    ada = _ada_call(c_pad, w_ada, b_ada).reshape(depth * ADA_ROWS * ADA_SLOTS, 1, d)
    cosf, sinf = _rope_tables(positions)

    w_in_b = w_in.astype(BF16)
    w_qkvg = w_in_b[:, :, :qkvg_cols]
    w_gate = w_in_b[:, :, qkvg_cols:]
    w_attn_br_b = w_attn_br.astype(BF16)
    w_conv_pw_b = w_conv_pw.astype(BF16)
    w_out_b = w_out.astype(BF16)
    w_up_b = w_up.astype(BF16)
    w_down_b = w_down.astype(BF16)
    row = lambda t: t.reshape(depth, 1, t.shape[-1])
    dw_w_tiles = jnp.broadcast_to(conv_dw_w[:, :, None, :], (depth, CONV_KERNEL, SUBLANES, CONV_WIDTH))

    for layer in range(depth):
        q, k, vt, kmean, conv_h = _inproj_call(layer, x, ada, w_qkvg, cosf, sinf, dw_w_tiles,
                                               row(conv_dw_b), row(conv_ln_g), row(conv_ln_b))
        attn_t = _attn_call(q, k.reshape(b, n_blk, MOBA_BLOCK, ATTN_WIDTH), vt,
                            kmean.reshape(b, n_blk, ATTN_WIDTH))
        x = _mix_call(layer, alpha, x, ada, attn_t, conv_h, w_gate, w_attn_br_b, w_conv_pw_b, w_out_b,
                      row(ln_mix_g), row(ln_mix_b))
        x = _ffn_call(layer, alpha, x, ada, w_up_b, w_down_b, row(ln_ffn_g), row(ln_ffn_b))
    return x
```

```python
import functools

import jax
import jax.numpy as jnp
from jax import lax
from jax.experimental import pallas as pl
from jax.experimental.pallas import tpu as pltpu

D_MODEL = 1024
N_HEADS = 8
HEAD_DIM = 64
ATTN_WIDTH = N_HEADS * HEAD_DIM
ROT_DIM = HEAD_DIM // 4
ROPE_THETA = 500000.0
MOBA_BLOCK = 256
MOBA_TOPK = 3
CONV_WIDTH = D_MODEL // 2
CONV_KERNEL = 31
D_FF = 4 * D_MODEL
LN_EPS = 1e-5
NEG_INF = -1e30

LANES = 128
HEADS_PER_GROUP = LANES // HEAD_DIM
SUBLANES = 8
BF16_ROWS = 16
CONV_HALO = 32
CONV_ROWS = 32
VT_ROWS = HEAD_DIM + BF16_ROWS
SCORE_LEAD = 3
QK_SCALE = HEAD_DIM ** -0.5 * 1.4426950408889634
ROW_TILE = 512
VMEM_LIMIT = 56 * 1024 * 1024

F32 = jnp.float32
BF16 = jnp.bfloat16


def _dot(a, b):
    return jnp.dot(a, b, preferred_element_type=F32)


def _dot_nt(a, b):
    return lax.dot_general(a, b, (((1,), (1,)), ((), ())), preferred_element_type=F32)


def _dot_tn(a, b):
    return lax.dot_general(a, b, (((0,), (0,)), ((), ())), preferred_element_type=F32)


def _layer_norm(r, g, b):
    mu = jnp.mean(r, axis=-1, keepdims=True)
    d = r - mu
    var = jnp.mean(d * d, axis=-1, keepdims=True)
    return d * lax.rsqrt(var + LN_EPS) * g + b


def _params(*semantics):
    return pltpu.CompilerParams(dimension_semantics=semantics, vmem_limit_bytes=VMEM_LIMIT)


def _ada_kernel(c_ref, w_ref, b_ref, o_ref):
    c = c_ref[...]
    c_act = c * jax.nn.sigmoid(c)
    o_ref[0] = _dot(c_act.astype(BF16), w_ref[0].astype(BF16)) + b_ref[0]


def _ada_call(c_pad, w_ada, b_ada):
    depth, d, six_d = w_ada.shape
    rows = c_pad.shape[0]
    n_chunks = six_d // d
    return pl.pallas_call(
        _ada_kernel,
        grid=(depth, n_chunks),
        in_specs=[
            pl.BlockSpec((rows, d), lambda l, j: (0, 0)),
            pl.BlockSpec((1, d, d), lambda l, j: (l, 0, j)),
            pl.BlockSpec((1, 1, d), lambda l, j: (l, 0, j)),
        ],
        out_specs=pl.BlockSpec((1, rows, d), lambda l, j: (l, 0, j)),
        out_shape=jax.ShapeDtypeStruct((depth, rows, six_d), F32),
        compiler_params=_params("parallel", "parallel"),
        name="ada",
    )(c_pad, w_ada, b_ada.reshape(depth, 1, six_d))


W_IN_BLOCK = 512


def _w_in_block(layer, j):
    return pl.BlockSpec((None, D_MODEL, W_IN_BLOCK), lambda bi, si: (layer, 0, j), pipeline_mode=pl.Buffered(1))


def _inproj_kernel(x_ref, sc_ref, sh_ref, wq_ref, wk_ref, wv_ref, wa_ref, wg_ref, cos_ref, sin_ref,
                   dww_ref, dwb_ref, lng_ref, lnb_ref,
                   q_ref, k_ref, vt_ref, km_ref, ch_ref, hext_ref, hsh_ref):
    tm = x_ref.shape[1]
    a_w = ATTN_WIDTH
    c_w = CONV_WIDTH

    @pl.when(pl.program_id(1) == 0)
    def _():
        hext_ref[0:CONV_HALO, :] = jnp.zeros((CONV_HALO, c_w), F32)

    u = x_ref[0] * (1.0 + sc_ref[0]) + sh_ref[0]
    cosf = cos_ref[0]
    sinf = sin_ref[0]
    lane = lax.broadcasted_iota(jnp.int32, (1, LANES), 1)
    first_half = (lane % HEAD_DIM) < (ROT_DIM // 2)

    def rope(t, c):
        xc = t[:, c * LANES:(c + 1) * LANES]
        partner = jnp.where(first_half,
                            pltpu.roll(xc, LANES - ROT_DIM // 2, 1),
                            pltpu.roll(xc, ROT_DIM // 2, 1))
        return xc * cosf + partner * sinf

    hext_ref[CONV_HALO:CONV_HALO + tm, :] = _dot(u, wa_ref[...]) * jax.nn.sigmoid(_dot(u, wg_ref[...]))

    q = _dot(u, wq_ref[...])
    for c in range(a_w // LANES):
        q_ref[0, :, c * LANES:(c + 1) * LANES] = (rope(q, c) * QK_SCALE).astype(BF16)

    k = _dot(u, wk_ref[...])
    for c in range(a_w // LANES):
        kc = rope(k, c)
        k_ref[0, :, c * LANES:(c + 1) * LANES] = kc.astype(BF16)
        for blk in range(tm // MOBA_BLOCK):
            km_ref[blk, :, c * LANES:(c + 1) * LANES] = jnp.mean(
                kc[blk * MOBA_BLOCK:(blk + 1) * MOBA_BLOCK], axis=0, keepdims=True)

    v = _dot(u, wv_ref[...])
    ones_row = (lax.broadcasted_iota(jnp.int32, (BF16_ROWS, MOBA_BLOCK), 0) == 0).astype(BF16)
    for blk in range(tm // MOBA_BLOCK):
        v_t = v[blk * MOBA_BLOCK:(blk + 1) * MOBA_BLOCK, :].T.astype(BF16)
        for h in range(N_HEADS):
            vt_ref[0, blk, h * VT_ROWS:h * VT_ROWS + HEAD_DIM, :] = v_t[h * HEAD_DIM:(h + 1) * HEAD_DIM, :]
            vt_ref[0, blk, h * VT_ROWS + HEAD_DIM:(h + 1) * VT_ROWS, :] = ones_row

    base = CONV_HALO - (CONV_KERNEL - 1)
    sh_rows = hsh_ref.shape[1]
    for s in range(1, SUBLANES):
        hsh_ref[s - 1] = hext_ref[s:s + sh_rows, :]
    lng = lng_ref[...]
    lnb = lnb_ref[...]
    dwb = dwb_ref[...]
    groups = CONV_ROWS // SUBLANES
    for r in range(0, tm, CONV_ROWS):
        acc = jnp.broadcast_to(dwb.reshape(1, 1, c_w), (groups, SUBLANES, c_w))
        for j in range(CONV_KERNEL):
            s, row = (base + j) % SUBLANES, r + (base + j) // SUBLANES * SUBLANES
            tap = hext_ref[row:row + CONV_ROWS, :] if s == 0 else hsh_ref[s - 1, row:row + CONV_ROWS, :]
            acc = acc + dww_ref[j][None] * tap.reshape(groups, SUBLANES, c_w)
        y = _layer_norm(acc.reshape(CONV_ROWS, c_w), lng, lnb)
        ch_ref[0, r:r + CONV_ROWS, :] = (y * jax.nn.sigmoid(y)).astype(BF16)

    hext_ref[0:CONV_HALO, :] = hext_ref[tm:tm + CONV_HALO, :]


def _inproj_call(layer, x, ada, w_in, cosf, sinf, dw_w, dw_b, cln_g, cln_b):
    b, s, d = x.shape
    tm = ROW_TILE
    n_s = s // tm
    blocks_per_tile = tm // MOBA_BLOCK
    n_blk = s // MOBA_BLOCK
    c_w = CONV_WIDTH

    def ada_map(slot):
        return lambda bi, si: (_ada_row(layer, bi, slot), 0, 0)

    in_specs = [
        pl.BlockSpec((1, tm, d), lambda bi, si: (bi, si, 0)),
        pl.BlockSpec((1, 1, d), ada_map(1)),
        pl.BlockSpec((1, 1, d), ada_map(0)),
        *[_w_in_block(layer, j) for j in range(5)],
        pl.BlockSpec((1, tm, LANES), lambda bi, si: (bi, si, 0)),
        pl.BlockSpec((1, tm, LANES), lambda bi, si: (bi, si, 0)),
        pl.BlockSpec((None, CONV_KERNEL, SUBLANES, c_w), lambda bi, si: (layer, 0, 0, 0)),
        pl.BlockSpec((None, 1, c_w), lambda bi, si: (layer, 0, 0)),
        pl.BlockSpec((None, 1, c_w), lambda bi, si: (layer, 0, 0)),
        pl.BlockSpec((None, 1, c_w), lambda bi, si: (layer, 0, 0)),
    ]
    out_specs = [
        pl.BlockSpec((1, tm, ATTN_WIDTH), lambda bi, si: (bi, si, 0)),
        pl.BlockSpec((1, tm, ATTN_WIDTH), lambda bi, si: (bi, si, 0)),
        pl.BlockSpec((1, blocks_per_tile, N_HEADS * VT_ROWS, MOBA_BLOCK), lambda bi, si: (bi, si, 0, 0)),
        pl.BlockSpec((blocks_per_tile, 1, ATTN_WIDTH), lambda bi, si: (bi * n_s + si, 0, 0)),
        pl.BlockSpec((1, tm, c_w), lambda bi, si: (bi, si, 0)),
    ]
    out_shape = [
        jax.ShapeDtypeStruct((b, s, ATTN_WIDTH), BF16),
        jax.ShapeDtypeStruct((b, s, ATTN_WIDTH), BF16),
        jax.ShapeDtypeStruct((b, n_blk, N_HEADS * VT_ROWS, MOBA_BLOCK), BF16),
        jax.ShapeDtypeStruct((b * n_blk, 1, ATTN_WIDTH), F32),
        jax.ShapeDtypeStruct((b, s, c_w), BF16),
    ]
    return pl.pallas_call(
        _inproj_kernel,
        grid=(b, n_s),
        in_specs=in_specs,
        out_specs=out_specs,
        out_shape=out_shape,
        scratch_shapes=[
            pltpu.VMEM((tm + CONV_HALO, c_w), F32),
            pltpu.VMEM((SUBLANES - 1, tm + CONV_HALO - SUBLANES, c_w), F32),
        ],
        compiler_params=_params("parallel", "arbitrary"),
        name="inproj",
    )(x, ada, ada, w_in, w_in, w_in, w_in, w_in, cosf, sinf, dw_w, dw_b, cln_g, cln_b)


ADA_SLOTS = 6
ADA_ROWS = 8


def _ada_row(layer, bi, slot):
    return (layer * ADA_ROWS + bi) * ADA_SLOTS + slot


def _attn_kernel(q_ref, k_ref, vt_ref, km_ref, o_ref, bias_ref, qh_ref, s_ref, smax_ref, m_ref, acc_ref):
    n_blk = k_ref.shape[1]
    tq = q_ref.shape[1]
    own = pl.program_id(1)
    blk_col = lax.broadcasted_iota(jnp.int32, (n_blk, 1), 0)
    valid = blk_col < own
    lane = lax.broadcasted_iota(jnp.int32, (1, LANES), 1)

    def group_lanes(h):
        g = h // HEADS_PER_GROUP
        return slice(g * LANES, (g + 1) * LANES)

    def vt_rows(h):
        return slice(h * VT_ROWS, (h + 1) * VT_ROWS)

    def score_block(slot, h, block):
        n = jnp.minimum(block, own)
        s_t = _dot_nt(k_ref[0, n, :, group_lanes(h)], qh_ref[h])
        s_ref[slot, h] = s_t
        smax_ref[slot, h] = jnp.max(s_t, axis=0, keepdims=True)

    for h in range(N_HEADS):
        in_head = (lane // HEAD_DIM) == (h % HEADS_PER_GROUP)
        q_g = q_ref[0, :, group_lanes(h)]
        q_h = jnp.where(in_head, q_g, jnp.zeros_like(q_g))
        qh_ref[h] = q_h
        km_h = jnp.where(in_head, km_ref[0, :, group_lanes(h)], 0.0)
        km_hi = km_h.astype(BF16)
        km_lo = (km_h - km_hi.astype(F32)).astype(BF16)
        gate = _dot_nt(km_hi, q_h) + _dot_nt(km_lo, q_h)
        gate = jnp.where(valid, gate, NEG_INF)
        rank = jnp.zeros((n_blk, tq), F32)
        for n2 in range(n_blk):
            row = gate[n2:n2 + 1, :]
            ahead = jnp.where(row > gate, 1.0, jnp.where(row == gate, (blk_col > n2).astype(F32), 0.0))
            rank = rank + ahead
        selected = jnp.where(rank < MOBA_TOPK, valid.astype(F32), 0.0)
        bias_ref[h] = jnp.where(selected > 0.0, 0.0, -NEG_INF)
        m_ref[h] = jnp.full((1, tq), NEG_INF, F32)
        acc_ref[h] = jnp.zeros((VT_ROWS, tq), F32)
    for h in range(N_HEADS):
        score_block(0, h, 0)

    def update(h, s_t, s_max, bias, vt_h):
        m = m_ref[h]
        m_new = jnp.maximum(m, s_max - bias)
        alpha = jnp.exp2(m - m_new)
        p_t = jnp.exp2(s_t - (m_new + bias)).astype(BF16)
        m_ref[h] = m_new
        acc_ref[h] = alpha * acc_ref[h] + _dot(vt_h, p_t)

    def one_block(slot, block):
        for h in range(SCORE_LEAD):
            score_block(1 - slot, h, block + 1)
        for h in range(N_HEADS):
            update(h, s_ref[slot, h], smax_ref[slot, h], bias_ref[h, pl.ds(block, 1), :],
                   vt_ref[0, block, vt_rows(h), :])
            if h + SCORE_LEAD < N_HEADS:
                score_block(1 - slot, h + SCORE_LEAD, block + 1)

    def two_blocks(j, carry):
        one_block(0, 2 * j)
        one_block(1, jnp.minimum(2 * j + 1, own))
        return carry

    lax.fori_loop(0, (own + 1) // 2, two_blocks, 0)

    key_pos = lax.broadcasted_iota(jnp.int32, (MOBA_BLOCK, tq), 0)
    q_pos = lax.broadcasted_iota(jnp.int32, (MOBA_BLOCK, tq), 1)
    causal = key_pos <= q_pos
    zero_bias = jnp.zeros((1, tq), F32)
    for h in range(N_HEADS):
        s_own = jnp.where(causal, s_ref[0, h], NEG_INF)
        update(h, s_own, jnp.max(s_own, axis=0, keepdims=True), zero_bias, vt_ref[0, own, vt_rows(h), :])
        acc = acc_ref[h]
        o_ref[0, h * HEAD_DIM:(h + 1) * HEAD_DIM, :] = (acc[0:HEAD_DIM] / acc[HEAD_DIM:HEAD_DIM + 1]).astype(BF16)


def _attn_call(q, k4, vt, kmean):
    b, s, _ = q.shape
    n_blk = s // MOBA_BLOCK
    tq = MOBA_BLOCK
    return pl.pallas_call(
        _attn_kernel,
        grid=(b, n_blk),
        in_specs=[
            pl.BlockSpec((1, tq, ATTN_WIDTH), lambda bi, i: (bi, i, 0)),
            pl.BlockSpec((1, n_blk, MOBA_BLOCK, ATTN_WIDTH), lambda bi, i: (bi, 0, 0, 0)),
            pl.BlockSpec((1, n_blk, N_HEADS * VT_ROWS, MOBA_BLOCK), lambda bi, i: (bi, 0, 0, 0)),
            pl.BlockSpec((1, n_blk, ATTN_WIDTH), lambda bi, i: (bi, 0, 0)),
        ],
        out_specs=pl.BlockSpec((1, ATTN_WIDTH, tq), lambda bi, i: (bi, 0, i)),
        out_shape=jax.ShapeDtypeStruct((b, ATTN_WIDTH, s), BF16),
        scratch_shapes=[
            pltpu.VMEM((N_HEADS, n_blk, tq), F32),
            pltpu.VMEM((N_HEADS, tq, LANES), BF16),
            pltpu.VMEM((2, N_HEADS, MOBA_BLOCK, tq), F32),
            pltpu.VMEM((2, N_HEADS, 1, tq), F32),
            pltpu.VMEM((N_HEADS, 1, tq), F32),
            pltpu.VMEM((N_HEADS, VT_ROWS, tq), F32),
        ],
        compiler_params=_params("parallel", "arbitrary"),
        name="moba_attn",
    )(q, k4, vt, kmean)


def _mix_kernel(alpha, x_ref, sc_ref, sh_ref, gt_ref, at_ref, ch_ref, wga0_ref, wga1_ref, wgc0_ref, wgc1_ref,
                wa_ref, wc_ref, wo_ref, lng_ref, lnb_ref, o_ref):
    x = x_ref[0]
    u = x * (1.0 + sc_ref[0]) + sh_ref[0]
    y_attn = _dot_tn(at_ref[0], wa_ref[...])
    y_conv = _dot(ch_ref[0], wc_ref[...])
    halves = []
    for i, (wga_ref, wgc_ref) in enumerate(((wga0_ref, wgc0_ref), (wga1_ref, wgc1_ref))):
        cols = slice(i * W_IN_BLOCK, (i + 1) * W_IN_BLOCK)
        halves.append(jax.nn.sigmoid(_dot(u, wga_ref[...])) * y_attn[:, cols]
                      + jax.nn.sigmoid(_dot(u, wgc_ref[...])) * y_conv[:, cols])
    y = _dot(jnp.concatenate(halves, axis=1), wo_ref[...])
    o_ref[0] = _layer_norm(alpha * x + (1.0 + gt_ref[0]) * y, lng_ref[...], lnb_ref[...])


def _mix_call(layer, alpha, x, ada, attn_t, conv_h, w_in, w_attn_br, w_conv_pw, w_out, ln_g, ln_b):
    b, s, d = x.shape
    tm = ROW_TILE

    def ada_map(slot):
        return lambda bi, si: (_ada_row(layer, bi, slot), 0, 0)

    def whole(arr):
        return pl.BlockSpec((None,) + arr.shape[1:], lambda bi, si: (layer,) + (0,) * (arr.ndim - 1),
                            pipeline_mode=pl.Buffered(1))

    return pl.pallas_call(
        functools.partial(_mix_kernel, alpha),
        grid=(b, s // tm),
        in_specs=[
            pl.BlockSpec((1, tm, d), lambda bi, si: (bi, si, 0)),
            pl.BlockSpec((1, 1, d), ada_map(1)),
            pl.BlockSpec((1, 1, d), ada_map(0)),
            pl.BlockSpec((1, 1, d), ada_map(2)),
            pl.BlockSpec((1, ATTN_WIDTH, tm), lambda bi, si: (bi, 0, si)),
            pl.BlockSpec((1, tm, CONV_WIDTH), lambda bi, si: (bi, si, 0)),
            *[_w_in_block(layer, j) for j in range(5, 9)],
            whole(w_attn_br), whole(w_conv_pw), whole(w_out), whole(ln_g), whole(ln_b),
        ],
        out_specs=pl.BlockSpec((1, tm, d), lambda bi, si: (bi, si, 0)),
        out_shape=jax.ShapeDtypeStruct((b, s, d), F32),
        compiler_params=_params("parallel", "parallel"),
        name="mix_out",
    )(x, ada, ada, ada, attn_t, conv_h, w_in, w_in, w_in, w_in, w_attn_br, w_conv_pw, w_out, ln_g, ln_b)


def _ffn_kernel(alpha, x_ref, sc_ref, sh_ref, gt_ref, wu_ref, wd_ref, lng_ref, lnb_ref, o_ref):
    d = D_MODEL
    x = x_ref[0]
    u = x * (1.0 + sc_ref[0]) + sh_ref[0]
    y = None
    for c in range(D_FF // d):
        h = jnp.maximum(_dot(u, wu_ref[:, c * d:(c + 1) * d]), 0.0)
        part = _dot(h * h, wd_ref[c * d:(c + 1) * d, :])
        y = part if y is None else y + part
    o_ref[0] = _layer_norm(alpha * x + (1.0 + gt_ref[0]) * y, lng_ref[...], lnb_ref[...])


def _ffn_call(layer, alpha, x, ada, w_up, w_down, ln_g, ln_b):
    b, s, d = x.shape
    tm = ROW_TILE

    def ada_map(slot):
        return lambda bi, si: (_ada_row(layer, bi, slot), 0, 0)

    def whole(arr):
        return pl.BlockSpec((None,) + arr.shape[1:], lambda bi, si: (layer,) + (0,) * (arr.ndim - 1),
                            pipeline_mode=pl.Buffered(1))

    return pl.pallas_call(
        functools.partial(_ffn_kernel, alpha),
        grid=(b, s // tm),
        in_specs=[
            pl.BlockSpec((1, tm, d), lambda bi, si: (bi, si, 0)),
            pl.BlockSpec((1, 1, d), ada_map(4)),
            pl.BlockSpec((1, 1, d), ada_map(3)),
            pl.BlockSpec((1, 1, d), ada_map(5)),
            whole(w_up), whole(w_down), whole(ln_g), whole(ln_b),
        ],
        out_specs=pl.BlockSpec((1, tm, d), lambda bi, si: (bi, si, 0)),
        out_shape=jax.ShapeDtypeStruct((b, s, d), F32),
        compiler_params=_params("parallel", "parallel"),
        name="ffn",
    )(x, ada, ada, ada, w_up, w_down, ln_g, ln_b)


def _rope_tables(positions):
    inv_freq = ROPE_THETA ** (-jnp.arange(0, ROT_DIM, 2, dtype=F32) / ROT_DIM)
    ang = positions.astype(F32)[..., None] * inv_freq
    cos, sin = jnp.cos(ang), jnp.sin(ang)
    rest = HEAD_DIM - ROT_DIM
    cos_h = jnp.concatenate([cos, cos, jnp.ones(cos.shape[:-1] + (rest,), F32)], -1)
    sin_h = jnp.concatenate([-sin, sin, jnp.zeros(sin.shape[:-1] + (rest,), F32)], -1)
    return jnp.tile(cos_h, (1, 1, HEADS_PER_GROUP)), jnp.tile(sin_h, (1, 1, HEADS_PER_GROUP))


def kernel(x, c, positions, w_in, w_attn_br, conv_dw_w, conv_dw_b, conv_ln_g, conv_ln_b, w_conv_pw, w_out,
           w_ada, b_ada, ln_mix_g, ln_mix_b, w_up, w_down, ln_ffn_g, ln_ffn_b):
    depth = w_in.shape[0]
    b, s, d = x.shape
    assert d == D_MODEL and s % ROW_TILE == 0 and ROW_TILE % MOBA_BLOCK == 0 and b <= ADA_ROWS
    alpha = (2 * depth) ** 0.25
    assert w_in.shape[2] == 9 * W_IN_BLOCK and ATTN_WIDTH == CONV_WIDTH == W_IN_BLOCK and d == 2 * W_IN_BLOCK
    n_blk = s // MOBA_BLOCK

    c_pad = jnp.zeros((ADA_ROWS, d), F32).at[:b].set(c)
    ada = _ada_call(c_pad, w_ada, b_ada).reshape(depth * ADA_ROWS * ADA_SLOTS, 1, d)
    cosf, sinf = _rope_tables(positions)

    w_attn_br_b = w_attn_br.astype(BF16)
    w_conv_pw_b = w_conv_pw.astype(BF16)
    row = lambda t: t.reshape(depth, 1, t.shape[-1])
    dw_w_tiles = jnp.broadcast_to(conv_dw_w[:, :, None, :], (depth, CONV_KERNEL, SUBLANES, CONV_WIDTH))

    for layer in range(depth):
        q, k, vt, kmean, conv_h = _inproj_call(layer, x, ada, w_in, cosf, sinf, dw_w_tiles,
                                               row(conv_dw_b), row(conv_ln_g), row(conv_ln_b))
        attn_t = _attn_call(q, k.reshape(b, n_blk, MOBA_BLOCK, ATTN_WIDTH), vt,
                            kmean.reshape(b, n_blk, ATTN_WIDTH))
        x = _mix_call(layer, alpha, x, ada, attn_t, conv_h, w_in, w_attn_br_b, w_conv_pw_b, w_out,
                      row(ln_mix_g), row(ln_mix_b))
        x = _ffn_call(layer, alpha, x, ada, w_up, w_down, row(ln_ffn_g), row(ln_ffn_b))
    return x
```

```python
import functools

import jax
import jax.numpy as jnp
from jax import lax
from jax.experimental import pallas as pl
from jax.experimental.pallas import tpu as pltpu

D_MODEL = 1024
N_HEADS = 8
HEAD_DIM = 64
ATTN_WIDTH = N_HEADS * HEAD_DIM
ROT_DIM = HEAD_DIM // 4
ROPE_THETA = 500000.0
MOBA_BLOCK = 256
MOBA_TOPK = 3
CONV_WIDTH = D_MODEL // 2
CONV_KERNEL = 31
D_FF = 4 * D_MODEL
LN_EPS = 1e-5
NEG_INF = -1e30

LANES = 128
HEADS_PER_GROUP = LANES // HEAD_DIM
SUBLANES = 8
BF16_ROWS = 16
CONV_HALO = 32
CONV_ROWS = 32
VT_ROWS = HEAD_DIM + BF16_ROWS
Q_BLOCKS = 2
SCORE_LEAD = 3
QK_SCALE = HEAD_DIM ** -0.5 * 1.4426950408889634
ROW_TILE = 512
VMEM_LIMIT = 56 * 1024 * 1024

F32 = jnp.float32
BF16 = jnp.bfloat16


def _dot(a, b):
    return jnp.dot(a, b, preferred_element_type=F32)


def _dot_nt(a, b):
    return lax.dot_general(a, b, (((1,), (1,)), ((), ())), preferred_element_type=F32)


def _dot_tn(a, b):
    return lax.dot_general(a, b, (((0,), (0,)), ((), ())), preferred_element_type=F32)


def _layer_norm(r, g, b):
    mu = jnp.mean(r, axis=-1, keepdims=True)
    d = r - mu
    var = jnp.mean(d * d, axis=-1, keepdims=True)
    return d * lax.rsqrt(var + LN_EPS) * g + b


def _params(*semantics):
    return pltpu.CompilerParams(dimension_semantics=semantics, vmem_limit_bytes=VMEM_LIMIT)


def _ada_kernel(c_ref, w_ref, b_ref, o_ref):
    c = c_ref[...]
    c_act = c * jax.nn.sigmoid(c)
    o_ref[0] = _dot(c_act.astype(BF16), w_ref[0].astype(BF16)) + b_ref[0]


def _ada_call(c_pad, w_ada, b_ada):
    depth, d, six_d = w_ada.shape
    rows = c_pad.shape[0]
    n_chunks = six_d // d
    return pl.pallas_call(
        _ada_kernel,
        grid=(depth, n_chunks),
        in_specs=[
            pl.BlockSpec((rows, d), lambda l, j: (0, 0)),
            pl.BlockSpec((1, d, d), lambda l, j: (l, 0, j)),
            pl.BlockSpec((1, 1, d), lambda l, j: (l, 0, j)),
        ],
        out_specs=pl.BlockSpec((1, rows, d), lambda l, j: (l, 0, j)),
        out_shape=jax.ShapeDtypeStruct((depth, rows, six_d), F32),
        compiler_params=_params("parallel", "parallel"),
        name="ada",
    )(c_pad, w_ada, b_ada.reshape(depth, 1, six_d))


W_IN_BLOCK = 512


def _w_in_block(layer, j):
    return pl.BlockSpec((None, D_MODEL, W_IN_BLOCK), lambda bi, si: (layer, 0, j), pipeline_mode=pl.Buffered(1))


def _inproj_kernel(x_ref, sc_ref, sh_ref, wq_ref, wk_ref, wv_ref, wa_ref, wg_ref, cos_ref, sin_ref,
                   dww_ref, dwb_ref, lng_ref, lnb_ref,
                   q_ref, k_ref, vt_ref, km_ref, ch_ref, hext_ref, hsh_ref):
    tm = x_ref.shape[1]
    a_w = ATTN_WIDTH
    c_w = CONV_WIDTH

    @pl.when(pl.program_id(1) == 0)
    def _():
        hext_ref[0:CONV_HALO, :] = jnp.zeros((CONV_HALO, c_w), F32)

    u = x_ref[0] * (1.0 + sc_ref[0]) + sh_ref[0]
    cosf = cos_ref[0]
    sinf = sin_ref[0]
    lane = lax.broadcasted_iota(jnp.int32, (1, LANES), 1)
    first_half = (lane % HEAD_DIM) < (ROT_DIM // 2)

    def rope(t, c):
        xc = t[:, c * LANES:(c + 1) * LANES]
        partner = jnp.where(first_half,
                            pltpu.roll(xc, LANES - ROT_DIM // 2, 1),
                            pltpu.roll(xc, ROT_DIM // 2, 1))
        return xc * cosf + partner * sinf

    hext_ref[CONV_HALO:CONV_HALO + tm, :] = _dot(u, wa_ref[...]) * jax.nn.sigmoid(_dot(u, wg_ref[...]))

    q = _dot(u, wq_ref[...])
    for c in range(a_w // LANES):
        q_ref[0, :, c * LANES:(c + 1) * LANES] = (rope(q, c) * QK_SCALE).astype(BF16)

    k = _dot(u, wk_ref[...])
    for c in range(a_w // LANES):
        kc = rope(k, c)
        k_ref[0, :, c * LANES:(c + 1) * LANES] = kc.astype(BF16)
        for blk in range(tm // MOBA_BLOCK):
            km_ref[blk, :, c * LANES:(c + 1) * LANES] = jnp.mean(
                kc[blk * MOBA_BLOCK:(blk + 1) * MOBA_BLOCK], axis=0, keepdims=True)

    v = _dot(u, wv_ref[...])
    ones_row = (lax.broadcasted_iota(jnp.int32, (BF16_ROWS, MOBA_BLOCK), 0) == 0).astype(BF16)
    for blk in range(tm // MOBA_BLOCK):
        v_t = v[blk * MOBA_BLOCK:(blk + 1) * MOBA_BLOCK, :].T.astype(BF16)
        for h in range(N_HEADS):
            vt_ref[0, blk, h * VT_ROWS:h * VT_ROWS + HEAD_DIM, :] = v_t[h * HEAD_DIM:(h + 1) * HEAD_DIM, :]
            vt_ref[0, blk, h * VT_ROWS + HEAD_DIM:(h + 1) * VT_ROWS, :] = ones_row

    base = CONV_HALO - (CONV_KERNEL - 1)
    sh_rows = hsh_ref.shape[1]
    for s in range(1, SUBLANES):
        hsh_ref[s - 1] = hext_ref[s:s + sh_rows, :]
    lng = lng_ref[...]
    lnb = lnb_ref[...]
    dwb = dwb_ref[...]
    groups = CONV_ROWS // SUBLANES
    for r in range(0, tm, CONV_ROWS):
        acc = jnp.broadcast_to(dwb.reshape(1, 1, c_w), (groups, SUBLANES, c_w))
        for j in range(CONV_KERNEL):
            s, row = (base + j) % SUBLANES, r + (base + j) // SUBLANES * SUBLANES
            tap = hext_ref[row:row + CONV_ROWS, :] if s == 0 else hsh_ref[s - 1, row:row + CONV_ROWS, :]
            acc = acc + dww_ref[j][None] * tap.reshape(groups, SUBLANES, c_w)
        y = _layer_norm(acc.reshape(CONV_ROWS, c_w), lng, lnb)
        ch_ref[0, r:r + CONV_ROWS, :] = (y * jax.nn.sigmoid(y)).astype(BF16)

    hext_ref[0:CONV_HALO, :] = hext_ref[tm:tm + CONV_HALO, :]


def _inproj_call(layer, x, ada, w_in, cosf, sinf, dw_w, dw_b, cln_g, cln_b):
    b, s, d = x.shape
    tm = ROW_TILE
    n_s = s // tm
    blocks_per_tile = tm // MOBA_BLOCK
    n_blk = s // MOBA_BLOCK
    c_w = CONV_WIDTH

    def ada_map(slot):
        return lambda bi, si: (_ada_row(layer, bi, slot), 0, 0)

    in_specs = [
        pl.BlockSpec((1, tm, d), lambda bi, si: (bi, si, 0)),
        pl.BlockSpec((1, 1, d), ada_map(1)),
        pl.BlockSpec((1, 1, d), ada_map(0)),
        *[_w_in_block(layer, j) for j in range(5)],
        pl.BlockSpec((1, tm, LANES), lambda bi, si: (bi, si, 0)),
        pl.BlockSpec((1, tm, LANES), lambda bi, si: (bi, si, 0)),
        pl.BlockSpec((None, CONV_KERNEL, SUBLANES, c_w), lambda bi, si: (layer, 0, 0, 0)),
        pl.BlockSpec((None, 1, c_w), lambda bi, si: (layer, 0, 0)),
        pl.BlockSpec((None, 1, c_w), lambda bi, si: (layer, 0, 0)),
        pl.BlockSpec((None, 1, c_w), lambda bi, si: (layer, 0, 0)),
    ]
    out_specs = [
        pl.BlockSpec((1, tm, ATTN_WIDTH), lambda bi, si: (bi, si, 0)),
        pl.BlockSpec((1, tm, ATTN_WIDTH), lambda bi, si: (bi, si, 0)),
        pl.BlockSpec((1, blocks_per_tile, N_HEADS * VT_ROWS, MOBA_BLOCK), lambda bi, si: (bi, si, 0, 0)),
        pl.BlockSpec((blocks_per_tile, 1, ATTN_WIDTH), lambda bi, si: (bi * n_s + si, 0, 0)),
        pl.BlockSpec((1, tm, c_w), lambda bi, si: (bi, si, 0)),
    ]
    out_shape = [
        jax.ShapeDtypeStruct((b, s, ATTN_WIDTH), BF16),
        jax.ShapeDtypeStruct((b, s, ATTN_WIDTH), BF16),
        jax.ShapeDtypeStruct((b, n_blk, N_HEADS * VT_ROWS, MOBA_BLOCK), BF16),
        jax.ShapeDtypeStruct((b * n_blk, 1, ATTN_WIDTH), F32),
        jax.ShapeDtypeStruct((b, s, c_w), BF16),
    ]
    return pl.pallas_call(
        _inproj_kernel,
        grid=(b, n_s),
        in_specs=in_specs,
        out_specs=out_specs,
        out_shape=out_shape,
        scratch_shapes=[
            pltpu.VMEM((tm + CONV_HALO, c_w), F32),
            pltpu.VMEM((SUBLANES - 1, tm + CONV_HALO - SUBLANES, c_w), F32),
        ],
        compiler_params=_params("parallel", "arbitrary"),
        name="inproj",
    )(x, ada, ada, w_in, w_in, w_in, w_in, w_in, cosf, sinf, dw_w, dw_b, cln_g, cln_b)


ADA_SLOTS = 6
ADA_ROWS = 8


def _ada_row(layer, bi, slot):
    return (layer * ADA_ROWS + bi) * ADA_SLOTS + slot


def _attn_kernel(q_ref, k_ref, vt_ref, km_ref, o_ref, bias_ref, qh_ref, s_ref, smax_ref, m_ref, acc_ref):
    n_blk = k_ref.shape[1]
    tq = q_ref.shape[1]
    first = Q_BLOCKS * pl.program_id(1)
    last = first + 1
    blk_col = lax.broadcasted_iota(jnp.int32, (n_blk, 1), 0)
    q_col = lax.broadcasted_iota(jnp.int32, (1, tq), 1)
    second_half = q_col >= MOBA_BLOCK
    valid = blk_col < first + second_half.astype(jnp.int32)
    lane = lax.broadcasted_iota(jnp.int32, (1, LANES), 1)
    late = slice(MOBA_BLOCK, tq)
    everyone = slice(0, tq)

    def group_lanes(h):
        g = h // HEADS_PER_GROUP
        return slice(g * LANES, (g + 1) * LANES)

    def vt_rows(h):
        return slice(h * VT_ROWS, (h + 1) * VT_ROWS)

    def score_block(slot, h, block):
        s_t = _dot_nt(k_ref[0, block, :, group_lanes(h)], qh_ref[h])
        s_ref[slot, h] = s_t
        smax_ref[slot, h] = jnp.max(s_t, axis=0, keepdims=True)

    def score_last_block(h):
        s_ref[1, h, :, late] = _dot_nt(k_ref[0, last, :, group_lanes(h)], qh_ref[h, late, :])

    for h in range(N_HEADS):
        in_head = (lane // HEAD_DIM) == (h % HEADS_PER_GROUP)
        q_g = q_ref[0, :, group_lanes(h)]
        q_h = jnp.where(in_head, q_g, jnp.zeros_like(q_g))
        qh_ref[h] = q_h
        km_h = jnp.where(in_head, km_ref[0, :, group_lanes(h)], 0.0)
        km_hi = km_h.astype(BF16)
        km_lo = (km_h - km_hi.astype(F32)).astype(BF16)
        gate = _dot_nt(km_hi, q_h) + _dot_nt(km_lo, q_h)
        gate = jnp.where(valid, gate, NEG_INF)
        rank = jnp.zeros((n_blk, tq), F32)
        for n2 in range(n_blk):
            row = gate[n2:n2 + 1, :]
            ahead = jnp.where(row > gate, 1.0, jnp.where(row == gate, (blk_col > n2).astype(F32), 0.0))
            rank = rank + ahead
        selected = jnp.where(rank < MOBA_TOPK, valid.astype(F32), 0.0)
        bias_ref[h] = jnp.where(selected > 0.0, 0.0, -NEG_INF)
        m_ref[h] = jnp.full((1, tq), NEG_INF, F32)
        acc_ref[h] = jnp.zeros((VT_ROWS, tq), F32)
    for h in range(N_HEADS):
        score_block(0, h, 0)

    def update(h, cols, s_t, s_max, bias, vt_h):
        m = m_ref[h, :, cols]
        m_new = jnp.maximum(m, s_max - bias)
        alpha = jnp.exp2(m - m_new)
        p_t = jnp.exp2(s_t - (m_new + bias)).astype(BF16)
        m_ref[h, :, cols] = m_new
        acc_ref[h, :, cols] = alpha * acc_ref[h, :, cols] + _dot(vt_h, p_t)

    def one_block(slot, block):
        for h in range(SCORE_LEAD):
            score_block(1 - slot, h, block + 1)
        for h in range(N_HEADS):
            update(h, everyone, s_ref[slot, h], smax_ref[slot, h], bias_ref[h, pl.ds(block, 1), :],
                   vt_ref[0, block, vt_rows(h), :])
            if h + SCORE_LEAD < N_HEADS:
                score_block(1 - slot, h + SCORE_LEAD, block + 1)

    def two_blocks(j, carry):
        one_block(0, 2 * j)
        one_block(1, 2 * j + 1)
        return carry

    lax.fori_loop(0, first // 2, two_blocks, 0)

    key_pos = lax.broadcasted_iota(jnp.int32, (MOBA_BLOCK, tq), 0)
    allowed = jnp.logical_or(second_half, key_pos <= q_col)
    for h in range(SCORE_LEAD):
        score_last_block(h)
    for h in range(N_HEADS):
        s_t = jnp.where(allowed, s_ref[0, h], NEG_INF)
        bias = jnp.where(second_half, bias_ref[h, pl.ds(first, 1), :], 0.0)
        update(h, everyone, s_t, jnp.max(s_t, axis=0, keepdims=True), bias, vt_ref[0, first, vt_rows(h), :])
        if h + SCORE_LEAD < N_HEADS:
            score_last_block(h + SCORE_LEAD)

    causal = (lax.broadcasted_iota(jnp.int32, (MOBA_BLOCK, MOBA_BLOCK), 0)
              <= lax.broadcasted_iota(jnp.int32, (MOBA_BLOCK, MOBA_BLOCK), 1))
    zero_bias = jnp.zeros((1, MOBA_BLOCK), F32)
    for h in range(N_HEADS):
        s_t = jnp.where(causal, s_ref[1, h, :, late], NEG_INF)
        update(h, late, s_t, jnp.max(s_t, axis=0, keepdims=True), zero_bias, vt_ref[0, last, vt_rows(h), :])
        acc = acc_ref[h]
        o_ref[0, h * HEAD_DIM:(h + 1) * HEAD_DIM, :] = (acc[0:HEAD_DIM] / acc[HEAD_DIM:HEAD_DIM + 1]).astype(BF16)


def _attn_call(q, k4, vt, kmean):
    b, s, _ = q.shape
    n_blk = s // MOBA_BLOCK
    tq = Q_BLOCKS * MOBA_BLOCK
    return pl.pallas_call(
        _attn_kernel,
        grid=(b, s // tq),
        in_specs=[
            pl.BlockSpec((1, tq, ATTN_WIDTH), lambda bi, i: (bi, i, 0)),
            pl.BlockSpec((1, n_blk, MOBA_BLOCK, ATTN_WIDTH), lambda bi, i: (bi, 0, 0, 0)),
            pl.BlockSpec((1, n_blk, N_HEADS * VT_ROWS, MOBA_BLOCK), lambda bi, i: (bi, 0, 0, 0)),
            pl.BlockSpec((1, n_blk, ATTN_WIDTH), lambda bi, i: (bi, 0, 0)),
        ],
        out_specs=pl.BlockSpec((1, ATTN_WIDTH, tq), lambda bi, i: (bi, 0, i)),
        out_shape=jax.ShapeDtypeStruct((b, ATTN_WIDTH, s), BF16),
        scratch_shapes=[
            pltpu.VMEM((N_HEADS, n_blk, tq), F32),
            pltpu.VMEM((N_HEADS, tq, LANES), BF16),
            pltpu.VMEM((2, N_HEADS, MOBA_BLOCK, tq), F32),
            pltpu.VMEM((2, N_HEADS, 1, tq), F32),
            pltpu.VMEM((N_HEADS, 1, tq), F32),
            pltpu.VMEM((N_HEADS, VT_ROWS, tq), F32),
        ],
        compiler_params=_params("parallel", "arbitrary"),
        name="moba_attn",
    )(q, k4, vt, kmean)


def _mix_kernel(alpha, x_ref, sc_ref, sh_ref, gt_ref, at_ref, ch_ref, wga0_ref, wga1_ref, wgc0_ref, wgc1_ref,
                wa_ref, wc_ref, wo_ref, lng_ref, lnb_ref, o_ref):
    x = x_ref[0]
    u = x * (1.0 + sc_ref[0]) + sh_ref[0]
    y_attn = _dot_tn(at_ref[0], wa_ref[...])
    y_conv = _dot(ch_ref[0], wc_ref[...])
    halves = []
    for i, (wga_ref, wgc_ref) in enumerate(((wga0_ref, wgc0_ref), (wga1_ref, wgc1_ref))):
        cols = slice(i * W_IN_BLOCK, (i + 1) * W_IN_BLOCK)
        halves.append(jax.nn.sigmoid(_dot(u, wga_ref[...])) * y_attn[:, cols]
                      + jax.nn.sigmoid(_dot(u, wgc_ref[...])) * y_conv[:, cols])
    y = _dot(jnp.concatenate(halves, axis=1), wo_ref[...])
    o_ref[0] = _layer_norm(alpha * x + (1.0 + gt_ref[0]) * y, lng_ref[...], lnb_ref[...])


def _mix_call(layer, alpha, x, ada, attn_t, conv_h, w_in, w_attn_br, w_conv_pw, w_out, ln_g, ln_b):
    b, s, d = x.shape
    tm = ROW_TILE

    def ada_map(slot):
        return lambda bi, si: (_ada_row(layer, bi, slot), 0, 0)

    def whole(arr):
        return pl.BlockSpec((None,) + arr.shape[1:], lambda bi, si: (layer,) + (0,) * (arr.ndim - 1),
                            pipeline_mode=pl.Buffered(1))

    return pl.pallas_call(
        functools.partial(_mix_kernel, alpha),
        grid=(b, s // tm),
        in_specs=[
            pl.BlockSpec((1, tm, d), lambda bi, si: (bi, si, 0)),
            pl.BlockSpec((1, 1, d), ada_map(1)),
            pl.BlockSpec((1, 1, d), ada_map(0)),
            pl.BlockSpec((1, 1, d), ada_map(2)),
            pl.BlockSpec((1, ATTN_WIDTH, tm), lambda bi, si: (bi, 0, si)),
            pl.BlockSpec((1, tm, CONV_WIDTH), lambda bi, si: (bi, si, 0)),
            *[_w_in_block(layer, j) for j in range(5, 9)],
            whole(w_attn_br), whole(w_conv_pw), whole(w_out), whole(ln_g), whole(ln_b),
        ],
        out_specs=pl.BlockSpec((1, tm, d), lambda bi, si: (bi, si, 0)),
        out_shape=jax.ShapeDtypeStruct((b, s, d), F32),
        compiler_params=_params("parallel", "parallel"),
        name="mix_out",
    )(x, ada, ada, ada, attn_t, conv_h, w_in, w_in, w_in, w_in, w_attn_br, w_conv_pw, w_out, ln_g, ln_b)


def _ffn_kernel(alpha, x_ref, sc_ref, sh_ref, gt_ref, wu_ref, wd_ref, lng_ref, lnb_ref, o_ref):
    d = D_MODEL
    x = x_ref[0]
    u = x * (1.0 + sc_ref[0]) + sh_ref[0]
    y = None
    for c in range(D_FF // d):
        h = jnp.maximum(_dot(u, wu_ref[:, c * d:(c + 1) * d]), 0.0)
        part = _dot(h * h, wd_ref[c * d:(c + 1) * d, :])
        y = part if y is None else y + part
    o_ref[0] = _layer_norm(alpha * x + (1.0 + gt_ref[0]) * y, lng_ref[...], lnb_ref[...])


def _ffn_call(layer, alpha, x, ada, w_up, w_down, ln_g, ln_b):
    b, s, d = x.shape
    tm = ROW_TILE

    def ada_map(slot):
        return lambda bi, si: (_ada_row(layer, bi, slot), 0, 0)

    def whole(arr):
        return pl.BlockSpec((None,) + arr.shape[1:], lambda bi, si: (layer,) + (0,) * (arr.ndim - 1),
                            pipeline_mode=pl.Buffered(1))

    return pl.pallas_call(
        functools.partial(_ffn_kernel, alpha),
        grid=(b, s // tm),
        in_specs=[
            pl.BlockSpec((1, tm, d), lambda bi, si: (bi, si, 0)),
            pl.BlockSpec((1, 1, d), ada_map(4)),
            pl.BlockSpec((1, 1, d), ada_map(3)),
            pl.BlockSpec((1, 1, d), ada_map(5)),
            whole(w_up), whole(w_down), whole(ln_g), whole(ln_b),
        ],
        out_specs=pl.BlockSpec((1, tm, d), lambda bi, si: (bi, si, 0)),
        out_shape=jax.ShapeDtypeStruct((b, s, d), F32),
        compiler_params=_params("parallel", "parallel"),
        name="ffn",
    )(x, ada, ada, ada, w_up, w_down, ln_g, ln_b)


def _rope_tables(positions):
    inv_freq = ROPE_THETA ** (-jnp.arange(0, ROT_DIM, 2, dtype=F32) / ROT_DIM)
    ang = positions.astype(F32)[..., None] * inv_freq
    cos, sin = jnp.cos(ang), jnp.sin(ang)
    rest = HEAD_DIM - ROT_DIM
    cos_h = jnp.concatenate([cos, cos, jnp.ones(cos.shape[:-1] + (rest,), F32)], -1)
    sin_h = jnp.concatenate([-sin, sin, jnp.zeros(sin.shape[:-1] + (rest,), F32)], -1)
    return jnp.tile(cos_h, (1, 1, HEADS_PER_GROUP)), jnp.tile(sin_h, (1, 1, HEADS_PER_GROUP))


def kernel(x, c, positions, w_in, w_attn_br, conv_dw_w, conv_dw_b, conv_ln_g, conv_ln_b, w_conv_pw, w_out,
           w_ada, b_ada, ln_mix_g, ln_mix_b, w_up, w_down, ln_ffn_g, ln_ffn_b):
    depth = w_in.shape[0]
    b, s, d = x.shape
    assert d == D_MODEL and s % ROW_TILE == 0 and ROW_TILE % MOBA_BLOCK == 0 and b <= ADA_ROWS
    assert s % (Q_BLOCKS * MOBA_BLOCK) == 0
    alpha = (2 * depth) ** 0.25
    assert w_in.shape[2] == 9 * W_IN_BLOCK and ATTN_WIDTH == CONV_WIDTH == W_IN_BLOCK and d == 2 * W_IN_BLOCK
    n_blk = s // MOBA_BLOCK

    c_pad = jnp.zeros((ADA_ROWS, d), F32).at[:b].set(c)
    ada = _ada_call(c_pad, w_ada, b_ada).reshape(depth * ADA_ROWS * ADA_SLOTS, 1, d)
    cosf, sinf = _rope_tables(positions)

    w_attn_br_b = w_attn_br.astype(BF16)
    w_conv_pw_b = w_conv_pw.astype(BF16)
    row = lambda t: t.reshape(depth, 1, t.shape[-1])
    dw_w_tiles = jnp.broadcast_to(conv_dw_w[:, :, None, :], (depth, CONV_KERNEL, SUBLANES, CONV_WIDTH))

    for layer in range(depth):
        q, k, vt, kmean, conv_h = _inproj_call(layer, x, ada, w_in, cosf, sinf, dw_w_tiles,
                                               row(conv_dw_b), row(conv_ln_g), row(conv_ln_b))
        attn_t = _attn_call(q, k.reshape(b, n_blk, MOBA_BLOCK, ATTN_WIDTH), vt,
                            kmean.reshape(b, n_blk, ATTN_WIDTH))
        x = _mix_call(layer, alpha, x, ada, attn_t, conv_h, w_in, w_attn_br_b, w_conv_pw_b, w_out,
                      row(ln_mix_g), row(ln_mix_b))
        x = _ffn_call(layer, alpha, x, ada, w_up, w_down, row(ln_ffn_g), row(ln_ffn_b))
    return x
```

```python
import functools

import jax
import jax.numpy as jnp
from jax import lax
from jax.experimental import pallas as pl
from jax.experimental.pallas import tpu as pltpu

D_MODEL = 1024
N_HEADS = 8
HEAD_DIM = 64
ATTN_WIDTH = N_HEADS * HEAD_DIM
ROT_DIM = HEAD_DIM // 4
ROPE_THETA = 500000.0
MOBA_BLOCK = 256
MOBA_TOPK = 3
CONV_WIDTH = D_MODEL // 2
CONV_KERNEL = 31
D_FF = 4 * D_MODEL
LN_EPS = 1e-5
NEG_INF = -1e30

LANES = 128
HEADS_PER_GROUP = LANES // HEAD_DIM
SUBLANES = 8
BF16_ROWS = 16
CONV_HALO = 32
CONV_ROWS = 32
VT_ROWS = HEAD_DIM + BF16_ROWS
Q_BLOCKS = 2
SCORE_LEAD = 2
QK_SCALE = HEAD_DIM ** -0.5 * 1.4426950408889634
ROW_TILE = 512
VMEM_LIMIT = 56 * 1024 * 1024

F32 = jnp.float32
BF16 = jnp.bfloat16


def _dot(a, b):
    return jnp.dot(a, b, preferred_element_type=F32)


def _dot_tn(a, b):
    return lax.dot_general(a, b, (((0,), (0,)), ((), ())), preferred_element_type=F32)


def _layer_norm(r, g, b):
    mu = jnp.mean(r, axis=-1, keepdims=True)
    d = r - mu
    var = jnp.mean(d * d, axis=-1, keepdims=True)
    return d * lax.rsqrt(var + LN_EPS) * g + b


def _params(*semantics):
    return pltpu.CompilerParams(dimension_semantics=semantics, vmem_limit_bytes=VMEM_LIMIT)


def _ada_kernel(c_ref, w_ref, b_ref, o_ref):
    c = c_ref[...]
    c_act = c * jax.nn.sigmoid(c)
    o_ref[0] = _dot(c_act.astype(BF16), w_ref[0].astype(BF16)) + b_ref[0]


def _ada_call(c_pad, w_ada, b_ada):
    depth, d, six_d = w_ada.shape
    rows = c_pad.shape[0]
    n_chunks = six_d // d
    return pl.pallas_call(
        _ada_kernel,
        grid=(depth, n_chunks),
        in_specs=[
            pl.BlockSpec((rows, d), lambda l, j: (0, 0)),
            pl.BlockSpec((1, d, d), lambda l, j: (l, 0, j)),
            pl.BlockSpec((1, 1, d), lambda l, j: (l, 0, j)),
        ],
        out_specs=pl.BlockSpec((1, rows, d), lambda l, j: (l, 0, j)),
        out_shape=jax.ShapeDtypeStruct((depth, rows, six_d), F32),
        compiler_params=_params("parallel", "parallel"),
        name="ada",
    )(c_pad, w_ada, b_ada.reshape(depth, 1, six_d))


W_IN_BLOCK = 512


def _w_in_block(layer, j):
    return pl.BlockSpec((None, D_MODEL, W_IN_BLOCK), lambda bi, si: (layer, 0, j), pipeline_mode=pl.Buffered(1))


def _inproj_kernel(x_ref, sc_ref, sh_ref, wq_ref, wk_ref, wv_ref, wa_ref, wg_ref, cos_ref, sin_ref,
                   dww_ref, dwb_ref, lng_ref, lnb_ref,
                   q_ref, k_ref, vt_ref, km_ref, ch_ref, hext_ref, hsh_ref):
    tm = x_ref.shape[1]
    a_w = ATTN_WIDTH
    c_w = CONV_WIDTH

    @pl.when(pl.program_id(1) == 0)
    def _():
        hext_ref[0:CONV_HALO, :] = jnp.zeros((CONV_HALO, c_w), F32)

    u = x_ref[0] * (1.0 + sc_ref[0]) + sh_ref[0]
    cosf = cos_ref[0]
    sinf = sin_ref[0]
    lane = lax.broadcasted_iota(jnp.int32, (1, LANES), 1)
    first_half = (lane % HEAD_DIM) < (ROT_DIM // 2)

    def rope(t, c):
        xc = t[:, c * LANES:(c + 1) * LANES]
        partner = jnp.where(first_half,
                            pltpu.roll(xc, LANES - ROT_DIM // 2, 1),
                            pltpu.roll(xc, ROT_DIM // 2, 1))
        return xc * cosf + partner * sinf

    hext_ref[CONV_HALO:CONV_HALO + tm, :] = _dot(u, wa_ref[...]) * jax.nn.sigmoid(_dot(u, wg_ref[...]))

    q = _dot(u, wq_ref[...])
    for c in range(a_w // LANES):
        q_ref[0, :, c * LANES:(c + 1) * LANES] = (rope(q, c) * QK_SCALE).astype(BF16)

    k = _dot(u, wk_ref[...])
    for c in range(a_w // LANES):
        kc = rope(k, c)
        k_ref[0, :, c * LANES:(c + 1) * LANES] = kc.astype(BF16)
        for blk in range(tm // MOBA_BLOCK):
            km_ref[blk, :, c * LANES:(c + 1) * LANES] = jnp.mean(
                kc[blk * MOBA_BLOCK:(blk + 1) * MOBA_BLOCK], axis=0, keepdims=True)

    v = _dot(u, wv_ref[...])
    ones_row = (lax.broadcasted_iota(jnp.int32, (BF16_ROWS, MOBA_BLOCK), 0) == 0).astype(BF16)
    for blk in range(tm // MOBA_BLOCK):
        v_t = v[blk * MOBA_BLOCK:(blk + 1) * MOBA_BLOCK, :].T.astype(BF16)
        for h in range(N_HEADS):
            vt_ref[0, blk, h * VT_ROWS:h * VT_ROWS + HEAD_DIM, :] = v_t[h * HEAD_DIM:(h + 1) * HEAD_DIM, :]
            vt_ref[0, blk, h * VT_ROWS + HEAD_DIM:(h + 1) * VT_ROWS, :] = ones_row

    base = CONV_HALO - (CONV_KERNEL - 1)
    sh_rows = hsh_ref.shape[1]
    for s in range(1, SUBLANES):
        hsh_ref[s - 1] = hext_ref[s:s + sh_rows, :]
    lng = lng_ref[...]
    lnb = lnb_ref[...]
    dwb = dwb_ref[...]
    groups = CONV_ROWS // SUBLANES
    for r in range(0, tm, CONV_ROWS):
        acc = jnp.broadcast_to(dwb.reshape(1, 1, c_w), (groups, SUBLANES, c_w))
        for j in range(CONV_KERNEL):
            s, row = (base + j) % SUBLANES, r + (base + j) // SUBLANES * SUBLANES
            tap = hext_ref[row:row + CONV_ROWS, :] if s == 0 else hsh_ref[s - 1, row:row + CONV_ROWS, :]
            acc = acc + dww_ref[j][None] * tap.reshape(groups, SUBLANES, c_w)
        y = _layer_norm(acc.reshape(CONV_ROWS, c_w), lng, lnb)
        ch_ref[0, r:r + CONV_ROWS, :] = (y * jax.nn.sigmoid(y)).astype(BF16)

    hext_ref[0:CONV_HALO, :] = hext_ref[tm:tm + CONV_HALO, :]


def _inproj_call(layer, x, ada, w_in, cosf, sinf, dw_w, dw_b, cln_g, cln_b):
    b, s, d = x.shape
    tm = ROW_TILE
    n_s = s // tm
    blocks_per_tile = tm // MOBA_BLOCK
    n_blk = s // MOBA_BLOCK
    c_w = CONV_WIDTH

    def ada_map(slot):
        return lambda bi, si: (_ada_row(layer, bi, slot), 0, 0)

    in_specs = [
        pl.BlockSpec((1, tm, d), lambda bi, si: (bi, si, 0)),
        pl.BlockSpec((1, 1, d), ada_map(1)),
        pl.BlockSpec((1, 1, d), ada_map(0)),
        *[_w_in_block(layer, j) for j in range(5)],
        pl.BlockSpec((1, tm, LANES), lambda bi, si: (bi, si, 0)),
        pl.BlockSpec((1, tm, LANES), lambda bi, si: (bi, si, 0)),
        pl.BlockSpec((None, CONV_KERNEL, SUBLANES, c_w), lambda bi, si: (layer, 0, 0, 0)),
        pl.BlockSpec((None, 1, c_w), lambda bi, si: (layer, 0, 0)),
        pl.BlockSpec((None, 1, c_w), lambda bi, si: (layer, 0, 0)),
        pl.BlockSpec((None, 1, c_w), lambda bi, si: (layer, 0, 0)),
    ]
    out_specs = [
        pl.BlockSpec((1, tm, ATTN_WIDTH), lambda bi, si: (bi, si, 0)),
        pl.BlockSpec((1, tm, ATTN_WIDTH), lambda bi, si: (bi, si, 0)),
        pl.BlockSpec((1, blocks_per_tile, N_HEADS * VT_ROWS, MOBA_BLOCK), lambda bi, si: (bi, si, 0, 0)),
        pl.BlockSpec((blocks_per_tile, 1, ATTN_WIDTH), lambda bi, si: (bi * n_s + si, 0, 0)),
        pl.BlockSpec((1, tm, c_w), lambda bi, si: (bi, si, 0)),
    ]
    out_shape = [
        jax.ShapeDtypeStruct((b, s, ATTN_WIDTH), BF16),
        jax.ShapeDtypeStruct((b, s, ATTN_WIDTH), BF16),
        jax.ShapeDtypeStruct((b, n_blk, N_HEADS * VT_ROWS, MOBA_BLOCK), BF16),
        jax.ShapeDtypeStruct((b * n_blk, 1, ATTN_WIDTH), F32),
        jax.ShapeDtypeStruct((b, s, c_w), BF16),
    ]
    return pl.pallas_call(
        _inproj_kernel,
        grid=(b, n_s),
        in_specs=in_specs,
        out_specs=out_specs,
        out_shape=out_shape,
        scratch_shapes=[
            pltpu.VMEM((tm + CONV_HALO, c_w), F32),
            pltpu.VMEM((SUBLANES - 1, tm + CONV_HALO - SUBLANES, c_w), F32),
        ],
        compiler_params=_params("parallel", "arbitrary"),
        name="inproj",
    )(x, ada, ada, w_in, w_in, w_in, w_in, w_in, cosf, sinf, dw_w, dw_b, cln_g, cln_b)


ADA_SLOTS = 6
ADA_ROWS = 8


def _ada_row(layer, bi, slot):
    return (layer * ADA_ROWS + bi) * ADA_SLOTS + slot


def _attn_kernel(q_ref, k_ref, vt_ref, km_ref, o_ref, bias_ref, qh_ref, s_ref, smax_ref, m_ref, acc_ref):
    n_blk = k_ref.shape[1]
    tq = q_ref.shape[1]
    first = Q_BLOCKS * pl.program_id(1)
    last = first + 1
    blk_col = lax.broadcasted_iota(jnp.int32, (n_blk, 1), 0)
    q_col = lax.broadcasted_iota(jnp.int32, (1, tq), 1)
    second_half = q_col >= MOBA_BLOCK
    valid = blk_col < first + second_half.astype(jnp.int32)
    lane = lax.broadcasted_iota(jnp.int32, (1, LANES), 1)
    late = slice(MOBA_BLOCK, tq)
    everyone = slice(0, tq)

    def group_lanes(h):
        g = h // HEADS_PER_GROUP
        return slice(g * LANES, (g + 1) * LANES)

    def vt_rows(h):
        return slice(h * VT_ROWS, (h + 1) * VT_ROWS)

    def score_block(slot, h, block):
        s_t = _dot(k_ref[0, block, :, group_lanes(h)], qh_ref[h])
        s_ref[slot, h] = s_t
        smax_ref[slot, h] = jnp.max(s_t, axis=0, keepdims=True)

    def score_last_block(h):
        s_ref[1, h, :, late] = _dot(k_ref[0, last, :, group_lanes(h)], qh_ref[h, :, late])

    dim_row = lax.broadcasted_iota(jnp.int32, (LANES, 1), 0)
    for h in range(N_HEADS):
        in_head = (lane // HEAD_DIM) == (h % HEADS_PER_GROUP)
        if h % HEADS_PER_GROUP == 0:
            q_gt = q_ref[0, :, group_lanes(h)].astype(F32).T
        q_h = jnp.where((dim_row // HEAD_DIM) == (h % HEADS_PER_GROUP), q_gt, 0.0).astype(BF16)
        qh_ref[h] = q_h
        km_h = jnp.where(in_head, km_ref[0, :, group_lanes(h)], 0.0)
        km_hi = km_h.astype(BF16)
        km_lo = (km_h - km_hi.astype(F32)).astype(BF16)
        gate = _dot(km_hi, q_h) + _dot(km_lo, q_h)
        gate = jnp.where(valid, gate, NEG_INF)
        rank = jnp.zeros((n_blk, tq), F32)
        for n2 in range(n_blk):
            row = gate[n2:n2 + 1, :]
            ahead = jnp.where(row > gate, 1.0, jnp.where(row == gate, (blk_col > n2).astype(F32), 0.0))
            rank = rank + ahead
        selected = jnp.where(rank < MOBA_TOPK, valid.astype(F32), 0.0)
        bias_ref[h] = jnp.where(selected > 0.0, 0.0, -NEG_INF)
        m_ref[h] = jnp.full((1, tq), NEG_INF, F32)
        acc_ref[h] = jnp.zeros((VT_ROWS, tq), F32)
    for h in range(N_HEADS):
        score_block(0, h, 0)

    def update(h, cols, s_t, s_max, bias, vt_h):
        m = m_ref[h, :, cols]
        m_new = jnp.maximum(m, s_max - bias)
        alpha = jnp.exp2(m - m_new)
        p_t = jnp.exp2(s_t - (m_new + bias)).astype(BF16)
        m_ref[h, :, cols] = m_new
        acc_ref[h, :, cols] = alpha * acc_ref[h, :, cols] + _dot(vt_h, p_t)

    def one_block(slot, block):
        for h in range(SCORE_LEAD):
            score_block(1 - slot, h, block + 1)
        for h in range(N_HEADS):
            update(h, everyone, s_ref[slot, h], smax_ref[slot, h], bias_ref[h, pl.ds(block, 1), :],
                   vt_ref[0, block, vt_rows(h), :])
            if h + SCORE_LEAD < N_HEADS:
                score_block(1 - slot, h + SCORE_LEAD, block + 1)

    def two_blocks(j, carry):
        one_block(0, 2 * j)
        one_block(1, 2 * j + 1)
        return carry

    lax.fori_loop(0, first // 2, two_blocks, 0)

    key_pos = lax.broadcasted_iota(jnp.int32, (MOBA_BLOCK, tq), 0)
    allowed = jnp.logical_or(second_half, key_pos <= q_col)
    for h in range(SCORE_LEAD):
        score_last_block(h)
    for h in range(N_HEADS):
        s_t = jnp.where(allowed, s_ref[0, h], NEG_INF)
        bias = jnp.where(second_half, bias_ref[h, pl.ds(first, 1), :], 0.0)
        update(h, everyone, s_t, jnp.max(s_t, axis=0, keepdims=True), bias, vt_ref[0, first, vt_rows(h), :])
        if h + SCORE_LEAD < N_HEADS:
            score_last_block(h + SCORE_LEAD)

    causal = (lax.broadcasted_iota(jnp.int32, (MOBA_BLOCK, MOBA_BLOCK), 0)
              <= lax.broadcasted_iota(jnp.int32, (MOBA_BLOCK, MOBA_BLOCK), 1))
    zero_bias = jnp.zeros((1, MOBA_BLOCK), F32)
    for h in range(N_HEADS):
        s_t = jnp.where(causal, s_ref[1, h, :, late], NEG_INF)
        update(h, late, s_t, jnp.max(s_t, axis=0, keepdims=True), zero_bias, vt_ref[0, last, vt_rows(h), :])
        acc = acc_ref[h]
        o_ref[0, h * HEAD_DIM:(h + 1) * HEAD_DIM, :] = (acc[0:HEAD_DIM] / acc[HEAD_DIM:HEAD_DIM + 1]).astype(BF16)


def _attn_call(q, k4, vt, kmean):
    b, s, _ = q.shape
    n_blk = s // MOBA_BLOCK
    tq = Q_BLOCKS * MOBA_BLOCK
    return pl.pallas_call(
        _attn_kernel,
        grid=(b, s // tq),
        in_specs=[
            pl.BlockSpec((1, tq, ATTN_WIDTH), lambda bi, i: (bi, i, 0)),
            pl.BlockSpec((1, n_blk, MOBA_BLOCK, ATTN_WIDTH), lambda bi, i: (bi, 0, 0, 0)),
            pl.BlockSpec((1, n_blk, N_HEADS * VT_ROWS, MOBA_BLOCK), lambda bi, i: (bi, 0, 0, 0)),
            pl.BlockSpec((1, n_blk, ATTN_WIDTH), lambda bi, i: (bi, 0, 0)),
        ],
        out_specs=pl.BlockSpec((1, ATTN_WIDTH, tq), lambda bi, i: (bi, 0, i)),
        out_shape=jax.ShapeDtypeStruct((b, ATTN_WIDTH, s), BF16),
        scratch_shapes=[
            pltpu.VMEM((N_HEADS, n_blk, tq), F32),
            pltpu.VMEM((N_HEADS, LANES, tq), BF16),
            pltpu.VMEM((2, N_HEADS, MOBA_BLOCK, tq), F32),
            pltpu.VMEM((2, N_HEADS, 1, tq), F32),
            pltpu.VMEM((N_HEADS, 1, tq), F32),
            pltpu.VMEM((N_HEADS, VT_ROWS, tq), F32),
        ],
        compiler_params=_params("parallel", "arbitrary"),
        name="moba_attn",
    )(q, k4, vt, kmean)


def _mix_kernel(alpha, x_ref, sc_ref, sh_ref, gt_ref, at_ref, ch_ref, wga0_ref, wga1_ref, wgc0_ref, wgc1_ref,
                wa_ref, wc_ref, wo_ref, lng_ref, lnb_ref, o_ref):
    x = x_ref[0]
    u = x * (1.0 + sc_ref[0]) + sh_ref[0]
    y_attn = _dot_tn(at_ref[0], wa_ref[...])
    y_conv = _dot(ch_ref[0], wc_ref[...])
    halves = []
    for i, (wga_ref, wgc_ref) in enumerate(((wga0_ref, wgc0_ref), (wga1_ref, wgc1_ref))):
        cols = slice(i * W_IN_BLOCK, (i + 1) * W_IN_BLOCK)
        halves.append(jax.nn.sigmoid(_dot(u, wga_ref[...])) * y_attn[:, cols]
                      + jax.nn.sigmoid(_dot(u, wgc_ref[...])) * y_conv[:, cols])
    y = _dot(jnp.concatenate(halves, axis=1), wo_ref[...])
    o_ref[0] = _layer_norm(alpha * x + (1.0 + gt_ref[0]) * y, lng_ref[...], lnb_ref[...])


def _mix_call(layer, alpha, x, ada, attn_t, conv_h, w_in, w_attn_br, w_conv_pw, w_out, ln_g, ln_b):
    b, s, d = x.shape
    tm = ROW_TILE

    def ada_map(slot):
        return lambda bi, si: (_ada_row(layer, bi, slot), 0, 0)

    def whole(arr):
        return pl.BlockSpec((None,) + arr.shape[1:], lambda bi, si: (layer,) + (0,) * (arr.ndim - 1),
                            pipeline_mode=pl.Buffered(1))

    return pl.pallas_call(
        functools.partial(_mix_kernel, alpha),
        grid=(b, s // tm),
        in_specs=[
            pl.BlockSpec((1, tm, d), lambda bi, si: (bi, si, 0)),
            pl.BlockSpec((1, 1, d), ada_map(1)),
            pl.BlockSpec((1, 1, d), ada_map(0)),
            pl.BlockSpec((1, 1, d), ada_map(2)),
            pl.BlockSpec((1, ATTN_WIDTH, tm), lambda bi, si: (bi, 0, si)),
            pl.BlockSpec((1, tm, CONV_WIDTH), lambda bi, si: (bi, si, 0)),
            *[_w_in_block(layer, j) for j in range(5, 9)],
            whole(w_attn_br), whole(w_conv_pw), whole(w_out), whole(ln_g), whole(ln_b),
        ],
        out_specs=pl.BlockSpec((1, tm, d), lambda bi, si: (bi, si, 0)),
        out_shape=jax.ShapeDtypeStruct((b, s, d), F32),
        compiler_params=_params("parallel", "parallel"),
        name="mix_out",
    )(x, ada, ada, ada, attn_t, conv_h, w_in, w_in, w_in, w_in, w_attn_br, w_conv_pw, w_out, ln_g, ln_b)


def _ffn_kernel(alpha, x_ref, sc_ref, sh_ref, gt_ref, wu_ref, wd_ref, lng_ref, lnb_ref, o_ref):
    d = D_MODEL
    x = x_ref[0]
    u = x * (1.0 + sc_ref[0]) + sh_ref[0]
    y = None
    for c in range(D_FF // d):
        h = jnp.maximum(_dot(u, wu_ref[:, c * d:(c + 1) * d]), 0.0)
        part = _dot(h * h, wd_ref[c * d:(c + 1) * d, :])
        y = part if y is None else y + part
    o_ref[0] = _layer_norm(alpha * x + (1.0 + gt_ref[0]) * y, lng_ref[...], lnb_ref[...])


def _ffn_call(layer, alpha, x, ada, w_up, w_down, ln_g, ln_b):
    b, s, d = x.shape
    tm = ROW_TILE

    def ada_map(slot):
        return lambda bi, si: (_ada_row(layer, bi, slot), 0, 0)

    def whole(arr):
        return pl.BlockSpec((None,) + arr.shape[1:], lambda bi, si: (layer,) + (0,) * (arr.ndim - 1),
                            pipeline_mode=pl.Buffered(1))

    return pl.pallas_call(
        functools.partial(_ffn_kernel, alpha),
        grid=(b, s // tm),
        in_specs=[
            pl.BlockSpec((1, tm, d), lambda bi, si: (bi, si, 0)),
            pl.BlockSpec((1, 1, d), ada_map(4)),
            pl.BlockSpec((1, 1, d), ada_map(3)),
            pl.BlockSpec((1, 1, d), ada_map(5)),
            whole(w_up), whole(w_down), whole(ln_g), whole(ln_b),
        ],
        out_specs=pl.BlockSpec((1, tm, d), lambda bi, si: (bi, si, 0)),
        out_shape=jax.ShapeDtypeStruct((b, s, d), F32),
        compiler_params=_params("parallel", "parallel"),
        name="ffn",
    )(x, ada, ada, ada, w_up, w_down, ln_g, ln_b)


def _rope_tables(positions):
    inv_freq = ROPE_THETA ** (-jnp.arange(0, ROT_DIM, 2, dtype=F32) / ROT_DIM)
    ang = positions.astype(F32)[..., None] * inv_freq
    cos, sin = jnp.cos(ang), jnp.sin(ang)
    rest = HEAD_DIM - ROT_DIM
    cos_h = jnp.concatenate([cos, cos, jnp.ones(cos.shape[:-1] + (rest,), F32)], -1)
    sin_h = jnp.concatenate([-sin, sin, jnp.zeros(sin.shape[:-1] + (rest,), F32)], -1)
    return jnp.tile(cos_h, (1, 1, HEADS_PER_GROUP)), jnp.tile(sin_h, (1, 1, HEADS_PER_GROUP))


def kernel(x, c, positions, w_in, w_attn_br, conv_dw_w, conv_dw_b, conv_ln_g, conv_ln_b, w_conv_pw, w_out,
           w_ada, b_ada, ln_mix_g, ln_mix_b, w_up, w_down, ln_ffn_g, ln_ffn_b):
    depth = w_in.shape[0]
    b, s, d = x.shape
    assert d == D_MODEL and s % ROW_TILE == 0 and ROW_TILE % MOBA_BLOCK == 0 and b <= ADA_ROWS
    assert s % (Q_BLOCKS * MOBA_BLOCK) == 0
    alpha = (2 * depth) ** 0.25
    assert w_in.shape[2] == 9 * W_IN_BLOCK and ATTN_WIDTH == CONV_WIDTH == W_IN_BLOCK and d == 2 * W_IN_BLOCK
    n_blk = s // MOBA_BLOCK

    c_pad = jnp.zeros((ADA_ROWS, d), F32).at[:b].set(c)
    ada = _ada_call(c_pad, w_ada, b_ada).reshape(depth * ADA_ROWS * ADA_SLOTS, 1, d)
    cosf, sinf = _rope_tables(positions)

    w_attn_br_b = w_attn_br.astype(BF16)
    w_conv_pw_b = w_conv_pw.astype(BF16)
    row = lambda t: t.reshape(depth, 1, t.shape[-1])
    dw_w_tiles = jnp.broadcast_to(conv_dw_w[:, :, None, :], (depth, CONV_KERNEL, SUBLANES, CONV_WIDTH))

    for layer in range(depth):
        q, k, vt, kmean, conv_h = _inproj_call(layer, x, ada, w_in, cosf, sinf, dw_w_tiles,
                                               row(conv_dw_b), row(conv_ln_g), row(conv_ln_b))
        attn_t = _attn_call(q, k.reshape(b, n_blk, MOBA_BLOCK, ATTN_WIDTH), vt,
                            kmean.reshape(b, n_blk, ATTN_WIDTH))
        x = _mix_call(layer, alpha, x, ada, attn_t, conv_h, w_in, w_attn_br_b, w_conv_pw_b, w_out,
                      row(ln_mix_g), row(ln_mix_b))
        x = _ffn_call(layer, alpha, x, ada, w_up, w_down, row(ln_ffn_g), row(ln_ffn_b))
    return x
```

```python
import functools

import jax
import jax.numpy as jnp
from jax import lax
from jax.experimental import pallas as pl
from jax.experimental.pallas import tpu as pltpu

D_MODEL = 1024
N_HEADS = 8
HEAD_DIM = 64
ATTN_WIDTH = N_HEADS * HEAD_DIM
ROT_DIM = HEAD_DIM // 4
ROPE_THETA = 500000.0
MOBA_BLOCK = 256
MOBA_TOPK = 3
CONV_WIDTH = D_MODEL // 2
CONV_KERNEL = 31
D_FF = 4 * D_MODEL
LN_EPS = 1e-5
NEG_INF = -1e30

LANES = 128
HEADS_PER_GROUP = LANES // HEAD_DIM
SUBLANES = 8
BF16_ROWS = 16
CONV_HALO = 32
CONV_ROWS = 32
VT_ROWS = HEAD_DIM + BF16_ROWS
Q_BLOCKS = 2
SCORE_LEAD = 1
QK_SCALE = HEAD_DIM ** -0.5 * 1.4426950408889634
ROW_TILE = 512
VMEM_LIMIT = 56 * 1024 * 1024

F32 = jnp.float32
BF16 = jnp.bfloat16


def _dot(a, b):
    return jnp.dot(a, b, preferred_element_type=F32)


def _dot_tn(a, b):
    return lax.dot_general(a, b, (((0,), (0,)), ((), ())), preferred_element_type=F32)


def _layer_norm(r, g, b):
    mu = jnp.mean(r, axis=-1, keepdims=True)
    d = r - mu
    var = jnp.mean(d * d, axis=-1, keepdims=True)
    return d * lax.rsqrt(var + LN_EPS) * g + b


def _params(*semantics):
    return pltpu.CompilerParams(dimension_semantics=semantics, vmem_limit_bytes=VMEM_LIMIT)


def _ada_kernel(c_ref, w_ref, b_ref, o_ref):
    c = c_ref[...]
    c_act = c * jax.nn.sigmoid(c)
    o_ref[0] = _dot(c_act.astype(BF16), w_ref[0].astype(BF16)) + b_ref[0]


def _ada_call(c_pad, w_ada, b_ada):
    depth, d, six_d = w_ada.shape
    rows = c_pad.shape[0]
    n_chunks = six_d // d
    return pl.pallas_call(
        _ada_kernel,
        grid=(depth, n_chunks),
        in_specs=[
            pl.BlockSpec((rows, d), lambda l, j: (0, 0)),
            pl.BlockSpec((1, d, d), lambda l, j: (l, 0, j)),
            pl.BlockSpec((1, 1, d), lambda l, j: (l, 0, j)),
        ],
        out_specs=pl.BlockSpec((1, rows, d), lambda l, j: (l, 0, j)),
        out_shape=jax.ShapeDtypeStruct((depth, rows, six_d), F32),
        compiler_params=_params("parallel", "parallel"),
        name="ada",
    )(c_pad, w_ada, b_ada.reshape(depth, 1, six_d))


MIX_STRIP = 256
W_IN_BLOCK = 512


def _w_in_block(layer, j):
    return pl.BlockSpec((None, D_MODEL, W_IN_BLOCK), lambda bi, si: (layer, 0, j), pipeline_mode=pl.Buffered(1))


def _inproj_kernel(x_ref, sc_ref, sh_ref, wq_ref, wk_ref, wv_ref, wa_ref, wg_ref, cos_ref, sin_ref,
                   dww_ref, dwb_ref, lng_ref, lnb_ref,
                   q_ref, k_ref, vt_ref, km_ref, ch_ref, hext_ref, hsh_ref):
    tm = x_ref.shape[1]
    a_w = ATTN_WIDTH
    c_w = CONV_WIDTH

    @pl.when(pl.program_id(1) == 0)
    def _():
        hext_ref[0:CONV_HALO, :] = jnp.zeros((CONV_HALO, c_w), F32)

    u = x_ref[0] * (1.0 + sc_ref[0]) + sh_ref[0]
    cosf = cos_ref[0]
    sinf = sin_ref[0]
    lane = lax.broadcasted_iota(jnp.int32, (1, LANES), 1)
    first_half = (lane % HEAD_DIM) < (ROT_DIM // 2)

    def rope(t, c):
        xc = t[:, c * LANES:(c + 1) * LANES]
        partner = jnp.where(first_half,
                            pltpu.roll(xc, LANES - ROT_DIM // 2, 1),
                            pltpu.roll(xc, ROT_DIM // 2, 1))
        return xc * cosf + partner * sinf

    hext_ref[CONV_HALO:CONV_HALO + tm, :] = _dot(u, wa_ref[...]) * jax.nn.sigmoid(_dot(u, wg_ref[...]))

    q = _dot(u, wq_ref[...])
    for c in range(a_w // LANES):
        q_ref[0, :, c * LANES:(c + 1) * LANES] = (rope(q, c) * QK_SCALE).astype(BF16)

    k = _dot(u, wk_ref[...])
    for c in range(a_w // LANES):
        kc = rope(k, c)
        k_ref[0, :, c * LANES:(c + 1) * LANES] = kc.astype(BF16)
        for blk in range(tm // MOBA_BLOCK):
            km_ref[blk, :, c * LANES:(c + 1) * LANES] = jnp.mean(
                kc[blk * MOBA_BLOCK:(blk + 1) * MOBA_BLOCK], axis=0, keepdims=True)

    v = _dot(u, wv_ref[...])
    ones_row = (lax.broadcasted_iota(jnp.int32, (BF16_ROWS, MOBA_BLOCK), 0) == 0).astype(BF16)
    for blk in range(tm // MOBA_BLOCK):
        v_t = v[blk * MOBA_BLOCK:(blk + 1) * MOBA_BLOCK, :].T.astype(BF16)
        for h in range(N_HEADS):
            vt_ref[0, blk, h * VT_ROWS:h * VT_ROWS + HEAD_DIM, :] = v_t[h * HEAD_DIM:(h + 1) * HEAD_DIM, :]
            vt_ref[0, blk, h * VT_ROWS + HEAD_DIM:(h + 1) * VT_ROWS, :] = ones_row

    base = CONV_HALO - (CONV_KERNEL - 1)
    sh_rows = hsh_ref.shape[1]
    for s in range(1, SUBLANES):
        hsh_ref[s - 1] = hext_ref[s:s + sh_rows, :]
    lng = lng_ref[...]
    lnb = lnb_ref[...]
    dwb = dwb_ref[...]
    groups = CONV_ROWS // SUBLANES
    for r in range(0, tm, CONV_ROWS):
        acc = jnp.broadcast_to(dwb.reshape(1, 1, c_w), (groups, SUBLANES, c_w))
        for j in range(CONV_KERNEL):
            s, row = (base + j) % SUBLANES, r + (base + j) // SUBLANES * SUBLANES
            tap = hext_ref[row:row + CONV_ROWS, :] if s == 0 else hsh_ref[s - 1, row:row + CONV_ROWS, :]
            acc = acc + dww_ref[j][None] * tap.reshape(groups, SUBLANES, c_w)
        y = _layer_norm(acc.reshape(CONV_ROWS, c_w), lng, lnb)
        ch_ref[0, r:r + CONV_ROWS, :] = (y * jax.nn.sigmoid(y)).astype(BF16)

    hext_ref[0:CONV_HALO, :] = hext_ref[tm:tm + CONV_HALO, :]


def _inproj_call(layer, x, ada, w_in, cosf, sinf, dw_w, dw_b, cln_g, cln_b):
    b, s, d = x.shape
    tm = ROW_TILE
    n_s = s // tm
    blocks_per_tile = tm // MOBA_BLOCK
    n_blk = s // MOBA_BLOCK
    c_w = CONV_WIDTH

    def ada_map(slot):
        return lambda bi, si: (_ada_row(layer, bi, slot), 0, 0)

    in_specs = [
        pl.BlockSpec((1, tm, d), lambda bi, si: (bi, si, 0)),
        pl.BlockSpec((1, 1, d), ada_map(1)),
        pl.BlockSpec((1, 1, d), ada_map(0)),
        *[_w_in_block(layer, j) for j in range(5)],
        pl.BlockSpec((1, tm, LANES), lambda bi, si: (bi, si, 0)),
        pl.BlockSpec((1, tm, LANES), lambda bi, si: (bi, si, 0)),
        pl.BlockSpec((None, CONV_KERNEL, SUBLANES, c_w), lambda bi, si: (layer, 0, 0, 0)),
        pl.BlockSpec((None, 1, c_w), lambda bi, si: (layer, 0, 0)),
        pl.BlockSpec((None, 1, c_w), lambda bi, si: (layer, 0, 0)),
        pl.BlockSpec((None, 1, c_w), lambda bi, si: (layer, 0, 0)),
    ]
    out_specs = [
        pl.BlockSpec((1, tm, ATTN_WIDTH), lambda bi, si: (bi, si, 0)),
        pl.BlockSpec((1, tm, ATTN_WIDTH), lambda bi, si: (bi, si, 0)),
        pl.BlockSpec((1, blocks_per_tile, N_HEADS * VT_ROWS, MOBA_BLOCK), lambda bi, si: (bi, si, 0, 0)),
        pl.BlockSpec((blocks_per_tile, 1, ATTN_WIDTH), lambda bi, si: (bi * n_s + si, 0, 0)),
        pl.BlockSpec((1, tm, c_w), lambda bi, si: (bi, si, 0)),
    ]
    out_shape = [
        jax.ShapeDtypeStruct((b, s, ATTN_WIDTH), BF16),
        jax.ShapeDtypeStruct((b, s, ATTN_WIDTH), BF16),
        jax.ShapeDtypeStruct((b, n_blk, N_HEADS * VT_ROWS, MOBA_BLOCK), BF16),
        jax.ShapeDtypeStruct((b * n_blk, 1, ATTN_WIDTH), F32),
        jax.ShapeDtypeStruct((b, s, c_w), BF16),
    ]
    return pl.pallas_call(
        _inproj_kernel,
        grid=(b, n_s),
        in_specs=in_specs,
        out_specs=out_specs,
        out_shape=out_shape,
        scratch_shapes=[
            pltpu.VMEM((tm + CONV_HALO, c_w), F32),
            pltpu.VMEM((SUBLANES - 1, tm + CONV_HALO - SUBLANES, c_w), F32),
        ],
        compiler_params=_params("parallel", "arbitrary"),
        name="inproj",
    )(x, ada, ada, w_in, w_in, w_in, w_in, w_in, cosf, sinf, dw_w, dw_b, cln_g, cln_b)


ADA_SLOTS = 6
ADA_ROWS = 8


def _ada_row(layer, bi, slot):
    return (layer * ADA_ROWS + bi) * ADA_SLOTS + slot


def _attn_kernel(q_ref, k_ref, vt_ref, km_ref, o_ref, bias_ref, qh_ref, s_ref, smax_ref, m_ref, acc_ref):
    n_blk = k_ref.shape[1]
    tq = q_ref.shape[1]
    first = Q_BLOCKS * pl.program_id(1)
    last = first + 1
    blk_col = lax.broadcasted_iota(jnp.int32, (n_blk, 1), 0)
    q_col = lax.broadcasted_iota(jnp.int32, (1, tq), 1)
    second_half = q_col >= MOBA_BLOCK
    valid = blk_col < first + second_half.astype(jnp.int32)
    lane = lax.broadcasted_iota(jnp.int32, (1, LANES), 1)
    late = slice(MOBA_BLOCK, tq)
    everyone = slice(0, tq)

    def group_lanes(h):
        g = h // HEADS_PER_GROUP
        return slice(g * LANES, (g + 1) * LANES)

    def vt_rows(h):
        return slice(h * VT_ROWS, (h + 1) * VT_ROWS)

    def score_block(slot, h, block):
        s_t = _dot(k_ref[0, block, :, group_lanes(h)], qh_ref[h])
        s_ref[slot, h] = s_t
        smax_ref[slot, h] = jnp.max(s_t, axis=0, keepdims=True)

    def score_last_block(h):
        s_ref[1, h, :, late] = _dot(k_ref[0, last, :, group_lanes(h)], qh_ref[h, :, late])

    dim_row = lax.broadcasted_iota(jnp.int32, (LANES, 1), 0)
    for h in range(N_HEADS):
        in_head = (lane // HEAD_DIM) == (h % HEADS_PER_GROUP)
        if h % HEADS_PER_GROUP == 0:
            q_gt = q_ref[0, :, group_lanes(h)].astype(F32).T
        q_h = jnp.where((dim_row // HEAD_DIM) == (h % HEADS_PER_GROUP), q_gt, 0.0).astype(BF16)
        qh_ref[h] = q_h
        km_h = jnp.where(in_head, km_ref[0, :, group_lanes(h)], 0.0)
        km_hi = km_h.astype(BF16)
        km_lo = (km_h - km_hi.astype(F32)).astype(BF16)
        gate = _dot(km_hi, q_h) + _dot(km_lo, q_h)
        gate = jnp.where(valid, gate, NEG_INF)
        rank = jnp.zeros((n_blk, tq), F32)
        for n2 in range(n_blk):
            row = gate[n2:n2 + 1, :]
            ahead = jnp.where(row > gate, 1.0, jnp.where(row == gate, (blk_col > n2).astype(F32), 0.0))
            rank = rank + ahead
        selected = jnp.where(rank < MOBA_TOPK, valid.astype(F32), 0.0)
        bias_ref[h] = jnp.where(selected > 0.0, 0.0, -NEG_INF)
        m_ref[h] = jnp.full((1, tq), NEG_INF, F32)
        acc_ref[h] = jnp.zeros((VT_ROWS, tq), F32)
    for h in range(N_HEADS):
        score_block(0, h, 0)

    def update(h, cols, s_t, s_max, bias, vt_h):
        m = m_ref[h, :, cols]
        m_new = jnp.maximum(m, s_max - bias)
        alpha = jnp.exp2(m - m_new)
        p_t = jnp.exp2(s_t - (m_new + bias)).astype(BF16)
        m_ref[h, :, cols] = m_new
        acc_ref[h, :, cols] = alpha * acc_ref[h, :, cols] + _dot(vt_h, p_t)

    def one_block(slot, block):
        for h in range(SCORE_LEAD):
            score_block(1 - slot, h, block + 1)
        for h in range(N_HEADS):
            update(h, everyone, s_ref[slot, h], smax_ref[slot, h], bias_ref[h, pl.ds(block, 1), :],
                   vt_ref[0, block, vt_rows(h), :])
            if h + SCORE_LEAD < N_HEADS:
                score_block(1 - slot, h + SCORE_LEAD, block + 1)

    def two_blocks(j, carry):
        one_block(0, 2 * j)
        one_block(1, 2 * j + 1)
        return carry

    lax.fori_loop(0, first // 2, two_blocks, 0)

    key_pos = lax.broadcasted_iota(jnp.int32, (MOBA_BLOCK, tq), 0)
    allowed = jnp.logical_or(second_half, key_pos <= q_col)
    for h in range(SCORE_LEAD):
        score_last_block(h)
    for h in range(N_HEADS):
        s_t = jnp.where(allowed, s_ref[0, h], NEG_INF)
        bias = jnp.where(second_half, bias_ref[h, pl.ds(first, 1), :], 0.0)
        update(h, everyone, s_t, jnp.max(s_t, axis=0, keepdims=True), bias, vt_ref[0, first, vt_rows(h), :])
        if h + SCORE_LEAD < N_HEADS:
            score_last_block(h + SCORE_LEAD)

    causal = (lax.broadcasted_iota(jnp.int32, (MOBA_BLOCK, MOBA_BLOCK), 0)
              <= lax.broadcasted_iota(jnp.int32, (MOBA_BLOCK, MOBA_BLOCK), 1))
    zero_bias = jnp.zeros((1, MOBA_BLOCK), F32)
    for h in range(N_HEADS):
        s_t = jnp.where(causal, s_ref[1, h, :, late], NEG_INF)
        update(h, late, s_t, jnp.max(s_t, axis=0, keepdims=True), zero_bias, vt_ref[0, last, vt_rows(h), :])
        acc = acc_ref[h]
        o_ref[0, h * HEAD_DIM:(h + 1) * HEAD_DIM, :] = (acc[0:HEAD_DIM] / acc[HEAD_DIM:HEAD_DIM + 1]).astype(BF16)


def _attn_call(q, k4, vt, kmean):
    b, s, _ = q.shape
    n_blk = s // MOBA_BLOCK
    tq = Q_BLOCKS * MOBA_BLOCK
    return pl.pallas_call(
        _attn_kernel,
        grid=(b, s // tq),
        in_specs=[
            pl.BlockSpec((1, tq, ATTN_WIDTH), lambda bi, i: (bi, i, 0)),
            pl.BlockSpec((1, n_blk, MOBA_BLOCK, ATTN_WIDTH), lambda bi, i: (bi, 0, 0, 0)),
            pl.BlockSpec((1, n_blk, N_HEADS * VT_ROWS, MOBA_BLOCK), lambda bi, i: (bi, 0, 0, 0)),
            pl.BlockSpec((1, n_blk, ATTN_WIDTH), lambda bi, i: (bi, 0, 0)),
        ],
        out_specs=pl.BlockSpec((1, ATTN_WIDTH, tq), lambda bi, i: (bi, 0, i)),
        out_shape=jax.ShapeDtypeStruct((b, ATTN_WIDTH, s), BF16),
        scratch_shapes=[
            pltpu.VMEM((N_HEADS, n_blk, tq), F32),
            pltpu.VMEM((N_HEADS, LANES, tq), BF16),
            pltpu.VMEM((2, N_HEADS, MOBA_BLOCK, tq), F32),
            pltpu.VMEM((2, N_HEADS, 1, tq), F32),
            pltpu.VMEM((N_HEADS, 1, tq), F32),
            pltpu.VMEM((N_HEADS, VT_ROWS, tq), F32),
        ],
        compiler_params=_params("parallel", "arbitrary"),
        name="moba_attn",
    )(q, k4, vt, kmean)


def _mix_kernel(alpha, x_ref, sc_ref, sh_ref, gt_ref, at_ref, ch_ref, wga0_ref, wga1_ref, wgc0_ref, wgc1_ref,
                wa_ref, wc_ref, wo_ref, lng_ref, lnb_ref, o_ref):
    x = x_ref[0]
    u = x * (1.0 + sc_ref[0]) + sh_ref[0]
    strips = []
    for i, (wga_ref, wgc_ref) in enumerate(((wga0_ref, wgc0_ref), (wga1_ref, wgc1_ref))):
        for j in range(W_IN_BLOCK // MIX_STRIP):
            loc = slice(j * MIX_STRIP, (j + 1) * MIX_STRIP)
            cols = slice(i * W_IN_BLOCK + j * MIX_STRIP, i * W_IN_BLOCK + (j + 1) * MIX_STRIP)
            strips.append(jax.nn.sigmoid(_dot(u, wga_ref[:, loc])) * _dot_tn(at_ref[0], wa_ref[:, cols])
                          + jax.nn.sigmoid(_dot(u, wgc_ref[:, loc])) * _dot(ch_ref[0], wc_ref[:, cols]))
    y = _dot(jnp.concatenate(strips, axis=1), wo_ref[...])
    o_ref[0] = _layer_norm(alpha * x + (1.0 + gt_ref[0]) * y, lng_ref[...], lnb_ref[...])


def _mix_call(layer, alpha, x, ada, attn_t, conv_h, w_in, w_attn_br, w_conv_pw, w_out, ln_g, ln_b):
    b, s, d = x.shape
    tm = ROW_TILE

    def ada_map(slot):
        return lambda bi, si: (_ada_row(layer, bi, slot), 0, 0)

    def whole(arr):
        return pl.BlockSpec((None,) + arr.shape[1:], lambda bi, si: (layer,) + (0,) * (arr.ndim - 1),
                            pipeline_mode=pl.Buffered(1))

    return pl.pallas_call(
        functools.partial(_mix_kernel, alpha),
        grid=(b, s // tm),
        in_specs=[
            pl.BlockSpec((1, tm, d), lambda bi, si: (bi, si, 0)),
            pl.BlockSpec((1, 1, d), ada_map(1)),
            pl.BlockSpec((1, 1, d), ada_map(0)),
            pl.BlockSpec((1, 1, d), ada_map(2)),
            pl.BlockSpec((1, ATTN_WIDTH, tm), lambda bi, si: (bi, 0, si)),
            pl.BlockSpec((1, tm, CONV_WIDTH), lambda bi, si: (bi, si, 0)),
            *[_w_in_block(layer, j) for j in range(5, 9)],
            whole(w_attn_br), whole(w_conv_pw), whole(w_out), whole(ln_g), whole(ln_b),
        ],
        out_specs=pl.BlockSpec((1, tm, d), lambda bi, si: (bi, si, 0)),
        out_shape=jax.ShapeDtypeStruct((b, s, d), F32),
        compiler_params=_params("parallel", "parallel"),
        name="mix_out",
    )(x, ada, ada, ada, attn_t, conv_h, w_in, w_in, w_in, w_in, w_attn_br, w_conv_pw, w_out, ln_g, ln_b)


def _ffn_kernel(alpha, x_ref, sc_ref, sh_ref, gt_ref, wu_ref, wd_ref, lng_ref, lnb_ref, o_ref):
    d = D_MODEL
    x = x_ref[0]
    u = x * (1.0 + sc_ref[0]) + sh_ref[0]
    y = None
    for c in range(D_FF // d):
        h = jnp.maximum(_dot(u, wu_ref[:, c * d:(c + 1) * d]), 0.0)
        part = _dot(h * h, wd_ref[c * d:(c + 1) * d, :])
        y = part if y is None else y + part
    o_ref[0] = _layer_norm(alpha * x + (1.0 + gt_ref[0]) * y, lng_ref[...], lnb_ref[...])


def _ffn_call(layer, alpha, x, ada, w_up, w_down, ln_g, ln_b):
    b, s, d = x.shape
    tm = ROW_TILE

    def ada_map(slot):
        return lambda bi, si: (_ada_row(layer, bi, slot), 0, 0)

    def whole(arr):
        return pl.BlockSpec((None,) + arr.shape[1:], lambda bi, si: (layer,) + (0,) * (arr.ndim - 1),
                            pipeline_mode=pl.Buffered(1))

    return pl.pallas_call(
        functools.partial(_ffn_kernel, alpha),
        grid=(b, s // tm),
        in_specs=[
            pl.BlockSpec((1, tm, d), lambda bi, si: (bi, si, 0)),
            pl.BlockSpec((1, 1, d), ada_map(4)),
            pl.BlockSpec((1, 1, d), ada_map(3)),
            pl.BlockSpec((1, 1, d), ada_map(5)),
            whole(w_up), whole(w_down), whole(ln_g), whole(ln_b),
        ],
        out_specs=pl.BlockSpec((1, tm, d), lambda bi, si: (bi, si, 0)),
        out_shape=jax.ShapeDtypeStruct((b, s, d), F32),
        compiler_params=_params("parallel", "parallel"),
        name="ffn",
    )(x, ada, ada, ada, w_up, w_down, ln_g, ln_b)


def _rope_tables(positions):
    inv_freq = ROPE_THETA ** (-jnp.arange(0, ROT_DIM, 2, dtype=F32) / ROT_DIM)
    ang = positions.astype(F32)[..., None] * inv_freq
    cos, sin = jnp.cos(ang), jnp.sin(ang)
    rest = HEAD_DIM - ROT_DIM
    cos_h = jnp.concatenate([cos, cos, jnp.ones(cos.shape[:-1] + (rest,), F32)], -1)
    sin_h = jnp.concatenate([-sin, sin, jnp.zeros(sin.shape[:-1] + (rest,), F32)], -1)
    return jnp.tile(cos_h, (1, 1, HEADS_PER_GROUP)), jnp.tile(sin_h, (1, 1, HEADS_PER_GROUP))


def kernel(x, c, positions, w_in, w_attn_br, conv_dw_w, conv_dw_b, conv_ln_g, conv_ln_b, w_conv_pw, w_out,
           w_ada, b_ada, ln_mix_g, ln_mix_b, w_up, w_down, ln_ffn_g, ln_ffn_b):
    depth = w_in.shape[0]
    b, s, d = x.shape
    assert d == D_MODEL and s % ROW_TILE == 0 and ROW_TILE % MOBA_BLOCK == 0 and b <= ADA_ROWS
    assert s % (Q_BLOCKS * MOBA_BLOCK) == 0
    alpha = (2 * depth) ** 0.25
    assert w_in.shape[2] == 9 * W_IN_BLOCK and ATTN_WIDTH == CONV_WIDTH == W_IN_BLOCK and d == 2 * W_IN_BLOCK
    n_blk = s // MOBA_BLOCK

    c_pad = jnp.zeros((ADA_ROWS, d), F32).at[:b].set(c)
    ada = _ada_call(c_pad, w_ada, b_ada).reshape(depth * ADA_ROWS * ADA_SLOTS, 1, d)
    cosf, sinf = _rope_tables(positions)

    w_attn_br_b = w_attn_br.astype(BF16)
    w_conv_pw_b = w_conv_pw.astype(BF16)
    row = lambda t: t.reshape(depth, 1, t.shape[-1])
    dw_w_tiles = jnp.broadcast_to(conv_dw_w[:, :, None, :], (depth, CONV_KERNEL, SUBLANES, CONV_WIDTH))

    for layer in range(depth):
        q, k, vt, kmean, conv_h = _inproj_call(layer, x, ada, w_in, cosf, sinf, dw_w_tiles,
                                               row(conv_dw_b), row(conv_ln_g), row(conv_ln_b))
        attn_t = _attn_call(q, k.reshape(b, n_blk, MOBA_BLOCK, ATTN_WIDTH), vt,
                            kmean.reshape(b, n_blk, ATTN_WIDTH))
        x = _mix_call(layer, alpha, x, ada, attn_t, conv_h, w_in, w_attn_br_b, w_conv_pw_b, w_out,
                      row(ln_mix_g), row(ln_mix_b))
        x = _ffn_call(layer, alpha, x, ada, w_up, w_down, row(ln_ffn_g), row(ln_ffn_b))
    return x
```

```python
import functools

import jax
import jax.numpy as jnp
from jax import lax
from jax.experimental import pallas as pl
from jax.experimental.pallas import tpu as pltpu

D_MODEL = 1024
N_HEADS = 8
HEAD_DIM = 64
ATTN_WIDTH = N_HEADS * HEAD_DIM
ROT_DIM = HEAD_DIM // 4
ROPE_THETA = 500000.0
MOBA_BLOCK = 256
MOBA_TOPK = 3
CONV_WIDTH = D_MODEL // 2
CONV_KERNEL = 31
D_FF = 4 * D_MODEL
LN_EPS = 1e-5
NEG_INF = -1e30

LANES = 128
HEADS_PER_GROUP = LANES // HEAD_DIM
SUBLANES = 8
BF16_ROWS = 16
CONV_HALO = 32
CONV_ROWS = 32
VT_ROWS = HEAD_DIM + BF16_ROWS
Q_BLOCKS = 2
SCORE_LEAD = 1
QK_SCALE = HEAD_DIM ** -0.5 * 1.4426950408889634
ROW_TILE = 512
INPROJ_TILE = 1024
VMEM_LIMIT = 56 * 1024 * 1024

F32 = jnp.float32
BF16 = jnp.bfloat16


def _dot(a, b):
    return jnp.dot(a, b, preferred_element_type=F32)


def _dot_tn(a, b):
    return lax.dot_general(a, b, (((0,), (0,)), ((), ())), preferred_element_type=F32)


def _layer_norm(r, g, b):
    mu = jnp.mean(r, axis=-1, keepdims=True)
    d = r - mu
    var = jnp.mean(d * d, axis=-1, keepdims=True)
    return d * lax.rsqrt(var + LN_EPS) * g + b


def _params(*semantics):
    return pltpu.CompilerParams(dimension_semantics=semantics, vmem_limit_bytes=VMEM_LIMIT)


def _ada_kernel(c_ref, w_ref, b_ref, o_ref):
    c = c_ref[...]
    c_act = c * jax.nn.sigmoid(c)
    o_ref[0] = _dot(c_act.astype(BF16), w_ref[0].astype(BF16)) + b_ref[0]


def _ada_call(c_pad, w_ada, b_ada):
    depth, d, six_d = w_ada.shape
    rows = c_pad.shape[0]
    n_chunks = six_d // d
    return pl.pallas_call(
        _ada_kernel,
        grid=(depth, n_chunks),
        in_specs=[
            pl.BlockSpec((rows, d), lambda l, j: (0, 0)),
            pl.BlockSpec((1, d, d), lambda l, j: (l, 0, j)),
            pl.BlockSpec((1, 1, d), lambda l, j: (l, 0, j)),
        ],
        out_specs=pl.BlockSpec((1, rows, d), lambda l, j: (l, 0, j)),
        out_shape=jax.ShapeDtypeStruct((depth, rows, six_d), F32),
        compiler_params=_params("parallel", "parallel"),
        name="ada",
    )(c_pad, w_ada, b_ada.reshape(depth, 1, six_d))


MIX_STRIP = 256
W_IN_BLOCK = 512


def _w_in_block(layer, j):
    return pl.BlockSpec((None, D_MODEL, W_IN_BLOCK), lambda bi, si: (layer, 0, j), pipeline_mode=pl.Buffered(1))


def _inproj_kernel(x_ref, sc_ref, sh_ref, wq_ref, wk_ref, wv_ref, wa_ref, wg_ref, cos_ref, sin_ref,
                   dww_ref, dwb_ref, lng_ref, lnb_ref,
                   q_ref, k_ref, vt_ref, km_ref, ch_ref, hext_ref, hsh_ref):
    tm = x_ref.shape[1]
    a_w = ATTN_WIDTH
    c_w = CONV_WIDTH

    @pl.when(pl.program_id(1) == 0)
    def _():
        hext_ref[0:CONV_HALO, :] = jnp.zeros((CONV_HALO, c_w), F32)

    u = x_ref[0] * (1.0 + sc_ref[0]) + sh_ref[0]
    cosf = cos_ref[0]
    sinf = sin_ref[0]
    lane = lax.broadcasted_iota(jnp.int32, (1, LANES), 1)
    first_half = (lane % HEAD_DIM) < (ROT_DIM // 2)

    def rope(t, c):
        xc = t[:, c * LANES:(c + 1) * LANES]
        partner = jnp.where(first_half,
                            pltpu.roll(xc, LANES - ROT_DIM // 2, 1),
                            pltpu.roll(xc, ROT_DIM // 2, 1))
        return xc * cosf + partner * sinf

    hext_ref[CONV_HALO:CONV_HALO + tm, :] = _dot(u, wa_ref[...]) * jax.nn.sigmoid(_dot(u, wg_ref[...]))

    q = _dot(u, wq_ref[...])
    for c in range(a_w // LANES):
        q_ref[0, :, c * LANES:(c + 1) * LANES] = (rope(q, c) * QK_SCALE).astype(BF16)

    k = _dot(u, wk_ref[...])
    for c in range(a_w // LANES):
        kc = rope(k, c)
        k_ref[0, :, c * LANES:(c + 1) * LANES] = kc.astype(BF16)
        for blk in range(tm // MOBA_BLOCK):
            km_ref[blk, :, c * LANES:(c + 1) * LANES] = jnp.mean(
                kc[blk * MOBA_BLOCK:(blk + 1) * MOBA_BLOCK], axis=0, keepdims=True)

    v = _dot(u, wv_ref[...])
    ones_row = (lax.broadcasted_iota(jnp.int32, (BF16_ROWS, MOBA_BLOCK), 0) == 0).astype(BF16)
    for blk in range(tm // MOBA_BLOCK):
        v_t = v[blk * MOBA_BLOCK:(blk + 1) * MOBA_BLOCK, :].T.astype(BF16)
        for h in range(N_HEADS):
            vt_ref[0, blk, h * VT_ROWS:h * VT_ROWS + HEAD_DIM, :] = v_t[h * HEAD_DIM:(h + 1) * HEAD_DIM, :]
            vt_ref[0, blk, h * VT_ROWS + HEAD_DIM:(h + 1) * VT_ROWS, :] = ones_row

    base = CONV_HALO - (CONV_KERNEL - 1)
    sh_rows = hsh_ref.shape[1]
    for s in range(1, SUBLANES):
        hsh_ref[s - 1] = hext_ref[s:s + sh_rows, :]
    lng = lng_ref[...]
    lnb = lnb_ref[...]
    dwb = dwb_ref[...]
    groups = CONV_ROWS // SUBLANES
    for r in range(0, tm, CONV_ROWS):
        acc = jnp.broadcast_to(dwb.reshape(1, 1, c_w), (groups, SUBLANES, c_w))
        for j in range(CONV_KERNEL):
            s, row = (base + j) % SUBLANES, r + (base + j) // SUBLANES * SUBLANES
            tap = hext_ref[row:row + CONV_ROWS, :] if s == 0 else hsh_ref[s - 1, row:row + CONV_ROWS, :]
            acc = acc + dww_ref[j][None] * tap.reshape(groups, SUBLANES, c_w)
        y = _layer_norm(acc.reshape(CONV_ROWS, c_w), lng, lnb)
        ch_ref[0, r:r + CONV_ROWS, :] = (y * jax.nn.sigmoid(y)).astype(BF16)

    hext_ref[0:CONV_HALO, :] = hext_ref[tm:tm + CONV_HALO, :]


def _inproj_call(layer, x, ada, w_in, cosf, sinf, dw_w, dw_b, cln_g, cln_b):
    b, s, d = x.shape
    tm = INPROJ_TILE
    n_s = s // tm
    blocks_per_tile = tm // MOBA_BLOCK
    n_blk = s // MOBA_BLOCK
    c_w = CONV_WIDTH

    def ada_map(slot):
        return lambda bi, si: (_ada_row(layer, bi, slot), 0, 0)

    in_specs = [
        pl.BlockSpec((1, tm, d), lambda bi, si: (bi, si, 0)),
        pl.BlockSpec((1, 1, d), ada_map(1)),
        pl.BlockSpec((1, 1, d), ada_map(0)),
        *[_w_in_block(layer, j) for j in range(5)],
        pl.BlockSpec((1, tm, LANES), lambda bi, si: (bi, si, 0)),
        pl.BlockSpec((1, tm, LANES), lambda bi, si: (bi, si, 0)),
        pl.BlockSpec((None, CONV_KERNEL, SUBLANES, c_w), lambda bi, si: (layer, 0, 0, 0)),
        pl.BlockSpec((None, 1, c_w), lambda bi, si: (layer, 0, 0)),
        pl.BlockSpec((None, 1, c_w), lambda bi, si: (layer, 0, 0)),
        pl.BlockSpec((None, 1, c_w), lambda bi, si: (layer, 0, 0)),
    ]
    out_specs = [
        pl.BlockSpec((1, tm, ATTN_WIDTH), lambda bi, si: (bi, si, 0)),
        pl.BlockSpec((1, tm, ATTN_WIDTH), lambda bi, si: (bi, si, 0)),
        pl.BlockSpec((1, blocks_per_tile, N_HEADS * VT_ROWS, MOBA_BLOCK), lambda bi, si: (bi, si, 0, 0)),
        pl.BlockSpec((blocks_per_tile, 1, ATTN_WIDTH), lambda bi, si: (bi * n_s + si, 0, 0)),
        pl.BlockSpec((1, tm, c_w), lambda bi, si: (bi, si, 0)),
    ]
    out_shape = [
        jax.ShapeDtypeStruct((b, s, ATTN_WIDTH), BF16),
        jax.ShapeDtypeStruct((b, s, ATTN_WIDTH), BF16),
        jax.ShapeDtypeStruct((b, n_blk, N_HEADS * VT_ROWS, MOBA_BLOCK), BF16),
        jax.ShapeDtypeStruct((b * n_blk, 1, ATTN_WIDTH), F32),
        jax.ShapeDtypeStruct((b, s, c_w), BF16),
    ]
    return pl.pallas_call(
        _inproj_kernel,
        grid=(b, n_s),
        in_specs=in_specs,
        out_specs=out_specs,
        out_shape=out_shape,
        scratch_shapes=[
            pltpu.VMEM((tm + CONV_HALO, c_w), F32),
            pltpu.VMEM((SUBLANES - 1, tm + CONV_HALO - SUBLANES, c_w), F32),
        ],
        compiler_params=_params("parallel", "arbitrary"),
        name="inproj",
    )(x, ada, ada, w_in, w_in, w_in, w_in, w_in, cosf, sinf, dw_w, dw_b, cln_g, cln_b)


ADA_SLOTS = 6
ADA_ROWS = 8


def _ada_row(layer, bi, slot):
    return (layer * ADA_ROWS + bi) * ADA_SLOTS + slot


def _attn_kernel(q_ref, k_ref, vt_ref, km_ref, o_ref, bias_ref, qh_ref, s_ref, smax_ref, m_ref, acc_ref):
    n_blk = k_ref.shape[1]
    tq = q_ref.shape[1]
    first = Q_BLOCKS * pl.program_id(1)
    last = first + 1
    blk_col = lax.broadcasted_iota(jnp.int32, (n_blk, 1), 0)
    q_col = lax.broadcasted_iota(jnp.int32, (1, tq), 1)
    second_half = q_col >= MOBA_BLOCK
    valid = blk_col < first + second_half.astype(jnp.int32)
    lane = lax.broadcasted_iota(jnp.int32, (1, LANES), 1)
    late = slice(MOBA_BLOCK, tq)
    everyone = slice(0, tq)

    def group_lanes(h):
        g = h // HEADS_PER_GROUP
        return slice(g * LANES, (g + 1) * LANES)

    def vt_rows(h):
        return slice(h * VT_ROWS, (h + 1) * VT_ROWS)

    def score_block(slot, h, block):
        s_t = _dot(k_ref[0, block, :, group_lanes(h)], qh_ref[h])
        s_ref[slot, h] = s_t
        smax_ref[slot, h] = jnp.max(s_t, axis=0, keepdims=True)

    def score_last_block(h):
        s_ref[1, h, :, late] = _dot(k_ref[0, last, :, group_lanes(h)], qh_ref[h, :, late])

    dim_row = lax.broadcasted_iota(jnp.int32, (LANES, 1), 0)
    for h in range(N_HEADS):
        in_head = (lane // HEAD_DIM) == (h % HEADS_PER_GROUP)
        if h % HEADS_PER_GROUP == 0:
            q_gt = q_ref[0, :, group_lanes(h)].astype(F32).T
        q_h = jnp.where((dim_row // HEAD_DIM) == (h % HEADS_PER_GROUP), q_gt, 0.0).astype(BF16)
        qh_ref[h] = q_h
        km_h = jnp.where(in_head, km_ref[0, :, group_lanes(h)], 0.0)
        km_hi = km_h.astype(BF16)
        km_lo = (km_h - km_hi.astype(F32)).astype(BF16)
        gate = _dot(km_hi, q_h) + _dot(km_lo, q_h)
        gate = jnp.where(valid, gate, NEG_INF)
        rank = jnp.zeros((n_blk, tq), F32)
        for n2 in range(n_blk):
            row = gate[n2:n2 + 1, :]
            ahead = jnp.where(row > gate, 1.0, jnp.where(row == gate, (blk_col > n2).astype(F32), 0.0))
            rank = rank + ahead
        selected = jnp.where(rank < MOBA_TOPK, valid.astype(F32), 0.0)
        bias_ref[h] = jnp.where(selected > 0.0, 0.0, -NEG_INF)
        m_ref[h] = jnp.full((1, tq), NEG_INF, F32)
        acc_ref[h] = jnp.zeros((VT_ROWS, tq), F32)
    for h in range(N_HEADS):
        score_block(0, h, 0)

    def update(h, cols, s_t, s_max, bias, vt_h):
        m = m_ref[h, :, cols]
        m_new = jnp.maximum(m, s_max - bias)
        alpha = jnp.exp2(m - m_new)
        p_t = jnp.exp2(s_t - (m_new + bias)).astype(BF16)
        m_ref[h, :, cols] = m_new
        acc_ref[h, :, cols] = alpha * acc_ref[h, :, cols] + _dot(vt_h, p_t)

    def one_block(slot, block):
        for h in range(SCORE_LEAD):
            score_block(1 - slot, h, block + 1)
        for h in range(N_HEADS):
            update(h, everyone, s_ref[slot, h], smax_ref[slot, h], bias_ref[h, pl.ds(block, 1), :],
                   vt_ref[0, block, vt_rows(h), :])
            if h + SCORE_LEAD < N_HEADS:
                score_block(1 - slot, h + SCORE_LEAD, block + 1)

    def two_blocks(j, carry):
        one_block(0, 2 * j)
        one_block(1, 2 * j + 1)
        return carry

    lax.fori_loop(0, first // 2, two_blocks, 0)

    key_pos = lax.broadcasted_iota(jnp.int32, (MOBA_BLOCK, tq), 0)
    allowed = jnp.logical_or(second_half, key_pos <= q_col)
    for h in range(SCORE_LEAD):
        score_last_block(h)
    for h in range(N_HEADS):
        s_t = jnp.where(allowed, s_ref[0, h], NEG_INF)
        bias = jnp.where(second_half, bias_ref[h, pl.ds(first, 1), :], 0.0)
        update(h, everyone, s_t, jnp.max(s_t, axis=0, keepdims=True), bias, vt_ref[0, first, vt_rows(h), :])
        if h + SCORE_LEAD < N_HEADS:
            score_last_block(h + SCORE_LEAD)

    causal = (lax.broadcasted_iota(jnp.int32, (MOBA_BLOCK, MOBA_BLOCK), 0)
              <= lax.broadcasted_iota(jnp.int32, (MOBA_BLOCK, MOBA_BLOCK), 1))
    zero_bias = jnp.zeros((1, MOBA_BLOCK), F32)
    for h in range(N_HEADS):
        s_t = jnp.where(causal, s_ref[1, h, :, late], NEG_INF)
        update(h, late, s_t, jnp.max(s_t, axis=0, keepdims=True), zero_bias, vt_ref[0, last, vt_rows(h), :])
        acc = acc_ref[h]
        o_ref[0, h * HEAD_DIM:(h + 1) * HEAD_DIM, :] = (acc[0:HEAD_DIM] / acc[HEAD_DIM:HEAD_DIM + 1]).astype(BF16)


def _attn_call(q, k4, vt, kmean):
    b, s, _ = q.shape
    n_blk = s // MOBA_BLOCK
    tq = Q_BLOCKS * MOBA_BLOCK
    return pl.pallas_call(
        _attn_kernel,
        grid=(b, s // tq),
        in_specs=[
            pl.BlockSpec((1, tq, ATTN_WIDTH), lambda bi, i: (bi, i, 0)),
            pl.BlockSpec((1, n_blk, MOBA_BLOCK, ATTN_WIDTH), lambda bi, i: (bi, 0, 0, 0)),
            pl.BlockSpec((1, n_blk, N_HEADS * VT_ROWS, MOBA_BLOCK), lambda bi, i: (bi, 0, 0, 0)),
            pl.BlockSpec((1, n_blk, ATTN_WIDTH), lambda bi, i: (bi, 0, 0)),
        ],
        out_specs=pl.BlockSpec((1, ATTN_WIDTH, tq), lambda bi, i: (bi, 0, i)),
        out_shape=jax.ShapeDtypeStruct((b, ATTN_WIDTH, s), BF16),
        scratch_shapes=[
            pltpu.VMEM((N_HEADS, n_blk, tq), F32),
            pltpu.VMEM((N_HEADS, LANES, tq), BF16),
            pltpu.VMEM((2, N_HEADS, MOBA_BLOCK, tq), F32),
            pltpu.VMEM((2, N_HEADS, 1, tq), F32),
            pltpu.VMEM((N_HEADS, 1, tq), F32),
            pltpu.VMEM((N_HEADS, VT_ROWS, tq), F32),
        ],
        compiler_params=_params("parallel", "arbitrary"),
        name="moba_attn",
    )(q, k4, vt, kmean)


def _mix_kernel(alpha, x_ref, sc_ref, sh_ref, gt_ref, at_ref, ch_ref, wga0_ref, wga1_ref, wgc0_ref, wgc1_ref,
                wa_ref, wc_ref, wo_ref, lng_ref, lnb_ref, wu_ref, wd_ref, o_ref, wub_ref, wdb_ref):
    wub_ref[...] = wu_ref[...].astype(BF16)
    wdb_ref[...] = wd_ref[...].astype(BF16)

    x = x_ref[0]
    u = x * (1.0 + sc_ref[0]) + sh_ref[0]
    strips = []
    for i, (wga_ref, wgc_ref) in enumerate(((wga0_ref, wgc0_ref), (wga1_ref, wgc1_ref))):
        for j in range(W_IN_BLOCK // MIX_STRIP):
            loc = slice(j * MIX_STRIP, (j + 1) * MIX_STRIP)
            cols = slice(i * W_IN_BLOCK + j * MIX_STRIP, i * W_IN_BLOCK + (j + 1) * MIX_STRIP)
            strips.append(jax.nn.sigmoid(_dot(u, wga_ref[:, loc])) * _dot_tn(at_ref[0], wa_ref[:, cols])
                          + jax.nn.sigmoid(_dot(u, wgc_ref[:, loc])) * _dot(ch_ref[0], wc_ref[:, cols]))
    y = _dot(jnp.concatenate(strips, axis=1), wo_ref[...])
    o_ref[0] = _layer_norm(alpha * x + (1.0 + gt_ref[0]) * y, lng_ref[...], lnb_ref[...])


def _mix_call(layer, alpha, x, ada, attn_t, conv_h, w_in, w_attn_br, w_conv_pw, w_out, ln_g, ln_b, w_up, w_down):
    b, s, d = x.shape
    tm = ROW_TILE
    n_s = s // tm
    n_steps = b * n_s
    d_ff = w_up.shape[2]
    assert d % (n_steps * BF16_ROWS) == 0 and d_ff % (n_steps * BF16_ROWS) == 0
    up_rows, down_rows = d // n_steps, d_ff // n_steps

    def ada_map(slot):
        return lambda bi, si: (_ada_row(layer, bi, slot), 0, 0)

    def whole(arr):
        return pl.BlockSpec((None,) + arr.shape[1:], lambda bi, si: (layer,) + (0,) * (arr.ndim - 1),
                            pipeline_mode=pl.Buffered(1))

    return pl.pallas_call(
        functools.partial(_mix_kernel, alpha),
        grid=(b, s // tm),
        in_specs=[
            pl.BlockSpec((1, tm, d), lambda bi, si: (bi, si, 0)),
            pl.BlockSpec((1, 1, d), ada_map(1)),
            pl.BlockSpec((1, 1, d), ada_map(0)),
            pl.BlockSpec((1, 1, d), ada_map(2)),
            pl.BlockSpec((1, ATTN_WIDTH, tm), lambda bi, si: (bi, 0, si)),
            pl.BlockSpec((1, tm, CONV_WIDTH), lambda bi, si: (bi, si, 0)),
            *[_w_in_block(layer, j) for j in range(5, 9)],
            whole(w_attn_br), whole(w_conv_pw), whole(w_out), whole(ln_g), whole(ln_b),
            pl.BlockSpec((None, up_rows, d_ff), lambda bi, si: (layer, bi * n_s + si, 0)),
            pl.BlockSpec((None, down_rows, d), lambda bi, si: (layer, bi * n_s + si, 0)),
        ],
        out_specs=[
            pl.BlockSpec((1, tm, d), lambda bi, si: (bi, si, 0)),
            pl.BlockSpec((up_rows, d_ff), lambda bi, si: (bi * n_s + si, 0)),
            pl.BlockSpec((down_rows, d), lambda bi, si: (bi * n_s + si, 0)),
        ],
        out_shape=[
            jax.ShapeDtypeStruct((b, s, d), F32),
            jax.ShapeDtypeStruct((d, d_ff), BF16),
            jax.ShapeDtypeStruct((d_ff, d), BF16),
        ],
        compiler_params=_params("parallel", "parallel"),
        name="mix_out",
    )(x, ada, ada, ada, attn_t, conv_h, w_in, w_in, w_in, w_in, w_attn_br, w_conv_pw, w_out, ln_g, ln_b,
      w_up, w_down)


def _ffn_kernel(alpha, x_ref, sc_ref, sh_ref, gt_ref, wu_ref, wd_ref, lng_ref, lnb_ref, o_ref):
    d = D_MODEL
    x = x_ref[0]
    u = (x * (1.0 + sc_ref[0]) + sh_ref[0]).astype(BF16)
    y = None
    for c in range(D_FF // d):
        h = jnp.maximum(_dot(u, wu_ref[:, c * d:(c + 1) * d]), 0.0)
        part = _dot((h * h).astype(BF16), wd_ref[c * d:(c + 1) * d, :])
        y = part if y is None else y + part
    o_ref[0] = _layer_norm(alpha * x + (1.0 + gt_ref[0]) * y, lng_ref[...], lnb_ref[...])


def _ffn_call(layer, alpha, x, ada, w_up_b, w_down_b, ln_g, ln_b):
    b, s, d = x.shape
    tm = ROW_TILE

    def ada_map(slot):
        return lambda bi, si: (_ada_row(layer, bi, slot), 0, 0)

    def whole(arr):
        return pl.BlockSpec((None,) + arr.shape[1:], lambda bi, si: (layer,) + (0,) * (arr.ndim - 1),
                            pipeline_mode=pl.Buffered(1))

    def resident(arr):
        return pl.BlockSpec(arr.shape, lambda bi, si: (0,) * arr.ndim, pipeline_mode=pl.Buffered(1))

    return pl.pallas_call(
        functools.partial(_ffn_kernel, alpha),
        grid=(b, s // tm),
        in_specs=[
            pl.BlockSpec((1, tm, d), lambda bi, si: (bi, si, 0)),
            pl.BlockSpec((1, 1, d), ada_map(4)),
            pl.BlockSpec((1, 1, d), ada_map(3)),
            pl.BlockSpec((1, 1, d), ada_map(5)),
            resident(w_up_b), resident(w_down_b), whole(ln_g), whole(ln_b),
        ],
        out_specs=pl.BlockSpec((1, tm, d), lambda bi, si: (bi, si, 0)),
        out_shape=jax.ShapeDtypeStruct((b, s, d), F32),
        compiler_params=_params("parallel", "parallel"),
        name="ffn",
    )(x, ada, ada, ada, w_up_b, w_down_b, ln_g, ln_b)


def _rope_tables(positions):
    inv_freq = ROPE_THETA ** (-jnp.arange(0, ROT_DIM, 2, dtype=F32) / ROT_DIM)
    ang = positions.astype(F32)[..., None] * inv_freq
    cos, sin = jnp.cos(ang), jnp.sin(ang)
    rest = HEAD_DIM - ROT_DIM
    cos_h = jnp.concatenate([cos, cos, jnp.ones(cos.shape[:-1] + (rest,), F32)], -1)
    sin_h = jnp.concatenate([-sin, sin, jnp.zeros(sin.shape[:-1] + (rest,), F32)], -1)
    return jnp.tile(cos_h, (1, 1, HEADS_PER_GROUP)), jnp.tile(sin_h, (1, 1, HEADS_PER_GROUP))


def kernel(x, c, positions, w_in, w_attn_br, conv_dw_w, conv_dw_b, conv_ln_g, conv_ln_b, w_conv_pw, w_out,
           w_ada, b_ada, ln_mix_g, ln_mix_b, w_up, w_down, ln_ffn_g, ln_ffn_b):
    depth = w_in.shape[0]
    b, s, d = x.shape
    assert d == D_MODEL and s % ROW_TILE == 0 and ROW_TILE % MOBA_BLOCK == 0 and b <= ADA_ROWS
    assert s % (Q_BLOCKS * MOBA_BLOCK) == 0 and s % INPROJ_TILE == 0 and INPROJ_TILE % MOBA_BLOCK == 0
    alpha = (2 * depth) ** 0.25
    assert w_in.shape[2] == 9 * W_IN_BLOCK and ATTN_WIDTH == CONV_WIDTH == W_IN_BLOCK and d == 2 * W_IN_BLOCK
    n_blk = s // MOBA_BLOCK

    c_pad = jnp.zeros((ADA_ROWS, d), F32).at[:b].set(c)
    ada = _ada_call(c_pad, w_ada, b_ada).reshape(depth * ADA_ROWS * ADA_SLOTS, 1, d)
    cosf, sinf = _rope_tables(positions)

    w_attn_br_b = w_attn_br.astype(BF16)
    w_conv_pw_b = w_conv_pw.astype(BF16)
    row = lambda t: t.reshape(depth, 1, t.shape[-1])
    dw_w_tiles = jnp.broadcast_to(conv_dw_w[:, :, None, :], (depth, CONV_KERNEL, SUBLANES, CONV_WIDTH))

    for layer in range(depth):
        q, k, vt, kmean, conv_h = _inproj_call(layer, x, ada, w_in, cosf, sinf, dw_w_tiles,
                                               row(conv_dw_b), row(conv_ln_g), row(conv_ln_b))
        attn_t = _attn_call(q, k.reshape(b, n_blk, MOBA_BLOCK, ATTN_WIDTH), vt,
                            kmean.reshape(b, n_blk, ATTN_WIDTH))
        x, w_up_b, w_down_b = _mix_call(layer, alpha, x, ada, attn_t, conv_h, w_in, w_attn_br_b, w_conv_pw_b, w_out,
                                        row(ln_mix_g), row(ln_mix_b), w_up, w_down)
        x = _ffn_call(layer, alpha, x, ada, w_up_b, w_down_b, row(ln_ffn_g), row(ln_ffn_b))
    return x
```

```python
import functools

import jax
import jax.numpy as jnp
from jax import lax
from jax.experimental import pallas as pl
from jax.experimental.pallas import tpu as pltpu

D_MODEL = 1024
N_HEADS = 8
HEAD_DIM = 64
ATTN_WIDTH = N_HEADS * HEAD_DIM
ROT_DIM = HEAD_DIM // 4
ROPE_THETA = 500000.0
MOBA_BLOCK = 256
MOBA_TOPK = 3
CONV_WIDTH = D_MODEL // 2
CONV_KERNEL = 31
D_FF = 4 * D_MODEL
LN_EPS = 1e-5
NEG_INF = -1e30

LANES = 128
HEADS_PER_GROUP = LANES // HEAD_DIM
SUBLANES = 8
BF16_ROWS = 16
CONV_HALO = 32
CONV_ROWS = 32
VT_ROWS = HEAD_DIM + BF16_ROWS
Q_BLOCKS = 2
SCORE_LEAD = 1
QK_SCALE = HEAD_DIM ** -0.5 * 1.4426950408889634
ROW_TILE = 1024
INPROJ_TILE = 1024
VMEM_LIMIT = 56 * 1024 * 1024

F32 = jnp.float32
BF16 = jnp.bfloat16


def _dot(a, b):
    return jnp.dot(a, b, preferred_element_type=F32)


def _dot_tn(a, b):
    return lax.dot_general(a, b, (((0,), (0,)), ((), ())), preferred_element_type=F32)


def _layer_norm(r, g, b):
    mu = jnp.mean(r, axis=-1, keepdims=True)
    d = r - mu
    var = jnp.mean(d * d, axis=-1, keepdims=True)
    return d * lax.rsqrt(var + LN_EPS) * g + b


def _params(*semantics):
    return pltpu.CompilerParams(dimension_semantics=semantics, vmem_limit_bytes=VMEM_LIMIT)


def _ada_kernel(c_ref, w_ref, b_ref, o_ref):
    c = c_ref[...]
    c_act = c * jax.nn.sigmoid(c)
    o_ref[0] = _dot(c_act.astype(BF16), w_ref[0].astype(BF16)) + b_ref[0]


def _ada_call(c_pad, w_ada, b_ada):
    depth, d, six_d = w_ada.shape
    rows = c_pad.shape[0]
    n_chunks = six_d // d
    return pl.pallas_call(
        _ada_kernel,
        grid=(depth, n_chunks),
        in_specs=[
            pl.BlockSpec((rows, d), lambda l, j: (0, 0)),
            pl.BlockSpec((1, d, d), lambda l, j: (l, 0, j)),
            pl.BlockSpec((1, 1, d), lambda l, j: (l, 0, j)),
        ],
        out_specs=pl.BlockSpec((1, rows, d), lambda l, j: (l, 0, j)),
        out_shape=jax.ShapeDtypeStruct((depth, rows, six_d), F32),
        compiler_params=_params("parallel", "parallel"),
        name="ada",
    )(c_pad, w_ada, b_ada.reshape(depth, 1, six_d))


MIX_STRIP = 256
W_IN_BLOCK = 512


def _w_in_block(layer, j):
    return pl.BlockSpec((None, D_MODEL, W_IN_BLOCK), lambda bi, si: (layer, 0, j), pipeline_mode=pl.Buffered(1))


def _inproj_kernel(x_ref, sc_ref, sh_ref, wq_ref, wk_ref, wv_ref, wa_ref, wg_ref, cos_ref, sin_ref,
                   dww_ref, dwb_ref, lng_ref, lnb_ref,
                   q_ref, k_ref, vt_ref, km_ref, ch_ref, hext_ref, hsh_ref):
    tm = x_ref.shape[1]
    a_w = ATTN_WIDTH
    c_w = CONV_WIDTH

    @pl.when(pl.program_id(1) == 0)
    def _():
        hext_ref[0:CONV_HALO, :] = jnp.zeros((CONV_HALO, c_w), F32)

    u = x_ref[0] * (1.0 + sc_ref[0]) + sh_ref[0]
    cosf = cos_ref[0]
    sinf = sin_ref[0]
    lane = lax.broadcasted_iota(jnp.int32, (1, LANES), 1)
    first_half = (lane % HEAD_DIM) < (ROT_DIM // 2)

    def rope(t, c):
        xc = t[:, c * LANES:(c + 1) * LANES]
        partner = jnp.where(first_half,
                            pltpu.roll(xc, LANES - ROT_DIM // 2, 1),
                            pltpu.roll(xc, ROT_DIM // 2, 1))
        return xc * cosf + partner * sinf

    hext_ref[CONV_HALO:CONV_HALO + tm, :] = _dot(u, wa_ref[...]) * jax.nn.sigmoid(_dot(u, wg_ref[...]))

    q = _dot(u, wq_ref[...])
    for c in range(a_w // LANES):
        q_ref[0, :, c * LANES:(c + 1) * LANES] = (rope(q, c) * QK_SCALE).astype(BF16)

    k = _dot(u, wk_ref[...])
    for c in range(a_w // LANES):
        kc = rope(k, c)
        k_ref[0, :, c * LANES:(c + 1) * LANES] = kc.astype(BF16)
        for blk in range(tm // MOBA_BLOCK):
            km_ref[blk, :, c * LANES:(c + 1) * LANES] = jnp.mean(
                kc[blk * MOBA_BLOCK:(blk + 1) * MOBA_BLOCK], axis=0, keepdims=True)

    v = _dot(u, wv_ref[...])
    ones_row = (lax.broadcasted_iota(jnp.int32, (BF16_ROWS, MOBA_BLOCK), 0) == 0).astype(BF16)
    for blk in range(tm // MOBA_BLOCK):
        v_t = v[blk * MOBA_BLOCK:(blk + 1) * MOBA_BLOCK, :].T.astype(BF16)
        for h in range(N_HEADS):
            vt_ref[0, blk, h * VT_ROWS:h * VT_ROWS + HEAD_DIM, :] = v_t[h * HEAD_DIM:(h + 1) * HEAD_DIM, :]
            vt_ref[0, blk, h * VT_ROWS + HEAD_DIM:(h + 1) * VT_ROWS, :] = ones_row

    base = CONV_HALO - (CONV_KERNEL - 1)
    sh_rows = hsh_ref.shape[1]
    for s in range(1, SUBLANES):
        hsh_ref[s - 1] = hext_ref[s:s + sh_rows, :]
    lng = lng_ref[...]
    lnb = lnb_ref[...]
    dwb = dwb_ref[...]
    groups = CONV_ROWS // SUBLANES
    for r in range(0, tm, CONV_ROWS):
        acc = jnp.broadcast_to(dwb.reshape(1, 1, c_w), (groups, SUBLANES, c_w))
        for j in range(CONV_KERNEL):
            s, row = (base + j) % SUBLANES, r + (base + j) // SUBLANES * SUBLANES
            tap = hext_ref[row:row + CONV_ROWS, :] if s == 0 else hsh_ref[s - 1, row:row + CONV_ROWS, :]
            acc = acc + dww_ref[j][None] * tap.reshape(groups, SUBLANES, c_w)
        y = _layer_norm(acc.reshape(CONV_ROWS, c_w), lng, lnb)
        ch_ref[0, r:r + CONV_ROWS, :] = (y * jax.nn.sigmoid(y)).astype(BF16)

    hext_ref[0:CONV_HALO, :] = hext_ref[tm:tm + CONV_HALO, :]


def _inproj_call(layer, x, ada, w_in, cosf, sinf, dw_w, dw_b, cln_g, cln_b):
    b, s, d = x.shape
    tm = INPROJ_TILE
    n_s = s // tm
    blocks_per_tile = tm // MOBA_BLOCK
    n_blk = s // MOBA_BLOCK
    c_w = CONV_WIDTH

    def ada_map(slot):
        return lambda bi, si: (_ada_row(layer, bi, slot), 0, 0)

    in_specs = [
        pl.BlockSpec((1, tm, d), lambda bi, si: (bi, si, 0)),
        pl.BlockSpec((1, 1, d), ada_map(1)),
        pl.BlockSpec((1, 1, d), ada_map(0)),
        *[_w_in_block(layer, j) for j in range(5)],
        pl.BlockSpec((1, tm, LANES), lambda bi, si: (bi, si, 0)),
        pl.BlockSpec((1, tm, LANES), lambda bi, si: (bi, si, 0)),
        pl.BlockSpec((None, CONV_KERNEL, SUBLANES, c_w), lambda bi, si: (layer, 0, 0, 0)),
        pl.BlockSpec((None, 1, c_w), lambda bi, si: (layer, 0, 0)),
        pl.BlockSpec((None, 1, c_w), lambda bi, si: (layer, 0, 0)),
        pl.BlockSpec((None, 1, c_w), lambda bi, si: (layer, 0, 0)),
    ]
    out_specs = [
        pl.BlockSpec((1, tm, ATTN_WIDTH), lambda bi, si: (bi, si, 0)),
        pl.BlockSpec((1, tm, ATTN_WIDTH), lambda bi, si: (bi, si, 0)),
        pl.BlockSpec((1, blocks_per_tile, N_HEADS * VT_ROWS, MOBA_BLOCK), lambda bi, si: (bi, si, 0, 0)),
        pl.BlockSpec((blocks_per_tile, 1, ATTN_WIDTH), lambda bi, si: (bi * n_s + si, 0, 0)),
        pl.BlockSpec((1, tm, c_w), lambda bi, si: (bi, si, 0)),
    ]
    out_shape = [
        jax.ShapeDtypeStruct((b, s, ATTN_WIDTH), BF16),
        jax.ShapeDtypeStruct((b, s, ATTN_WIDTH), BF16),
        jax.ShapeDtypeStruct((b, n_blk, N_HEADS * VT_ROWS, MOBA_BLOCK), BF16),
        jax.ShapeDtypeStruct((b * n_blk, 1, ATTN_WIDTH), F32),
        jax.ShapeDtypeStruct((b, s, c_w), BF16),
    ]
    return pl.pallas_call(
        _inproj_kernel,
        grid=(b, n_s),
        in_specs=in_specs,
        out_specs=out_specs,
        out_shape=out_shape,
        scratch_shapes=[
            pltpu.VMEM((tm + CONV_HALO, c_w), F32),
            pltpu.VMEM((SUBLANES - 1, tm + CONV_HALO - SUBLANES, c_w), F32),
        ],
        compiler_params=_params("parallel", "arbitrary"),
        name="inproj",
    )(x, ada, ada, w_in, w_in, w_in, w_in, w_in, cosf, sinf, dw_w, dw_b, cln_g, cln_b)


ADA_SLOTS = 6
ADA_ROWS = 8


def _ada_row(layer, bi, slot):
    return (layer * ADA_ROWS + bi) * ADA_SLOTS + slot


def _attn_kernel(q_ref, k_ref, vt_ref, km_ref, o_ref, bias_ref, qh_ref, s_ref, smax_ref, m_ref, acc_ref):
    n_blk = k_ref.shape[1]
    tq = q_ref.shape[1]
    first = Q_BLOCKS * pl.program_id(1)
    last = first + 1
    blk_col = lax.broadcasted_iota(jnp.int32, (n_blk, 1), 0)
    q_col = lax.broadcasted_iota(jnp.int32, (1, tq), 1)
    second_half = q_col >= MOBA_BLOCK
    valid = blk_col < first + second_half.astype(jnp.int32)
    lane = lax.broadcasted_iota(jnp.int32, (1, LANES), 1)
    late = slice(MOBA_BLOCK, tq)
    everyone = slice(0, tq)

    def group_lanes(h):
        g = h // HEADS_PER_GROUP
        return slice(g * LANES, (g + 1) * LANES)

    def vt_rows(h):
        return slice(h * VT_ROWS, (h + 1) * VT_ROWS)

    def score_block(slot, h, block):
        s_t = _dot(k_ref[0, block, :, group_lanes(h)], qh_ref[h])
        s_ref[slot, h] = s_t
        smax_ref[slot, h] = jnp.max(s_t, axis=0, keepdims=True)

    def score_last_block(h):
        s_ref[1, h, :, late] = _dot(k_ref[0, last, :, group_lanes(h)], qh_ref[h, :, late])

    dim_row = lax.broadcasted_iota(jnp.int32, (LANES, 1), 0)
    for h in range(N_HEADS):
        in_head = (lane // HEAD_DIM) == (h % HEADS_PER_GROUP)
        if h % HEADS_PER_GROUP == 0:
            q_gt = q_ref[0, :, group_lanes(h)].astype(F32).T
        q_h = jnp.where((dim_row // HEAD_DIM) == (h % HEADS_PER_GROUP), q_gt, 0.0).astype(BF16)
        qh_ref[h] = q_h
        km_h = jnp.where(in_head, km_ref[0, :, group_lanes(h)], 0.0)
        km_hi = km_h.astype(BF16)
        km_lo = (km_h - km_hi.astype(F32)).astype(BF16)
        gate = _dot(km_hi, q_h) + _dot(km_lo, q_h)
        gate = jnp.where(valid, gate, NEG_INF)
        rank = jnp.zeros((n_blk, tq), F32)
        for n2 in range(n_blk):
            row = gate[n2:n2 + 1, :]
            ahead = jnp.where(row > gate, 1.0, jnp.where(row == gate, (blk_col > n2).astype(F32), 0.0))
            rank = rank + ahead
        selected = jnp.where(rank < MOBA_TOPK, valid.astype(F32), 0.0)
        bias_ref[h] = jnp.where(selected > 0.0, 0.0, -NEG_INF)
        m_ref[h] = jnp.full((1, tq), NEG_INF, F32)
        acc_ref[h] = jnp.zeros((VT_ROWS, tq), F32)
    for h in range(N_HEADS):
        score_block(0, h, 0)

    def update(h, cols, s_t, s_max, bias, vt_h):
        m = m_ref[h, :, cols]
        m_new = jnp.maximum(m, s_max - bias)
        alpha = jnp.exp2(m - m_new)
        p_t = jnp.exp2(s_t - (m_new + bias)).astype(BF16)
        m_ref[h, :, cols] = m_new
        acc_ref[h, :, cols] = alpha * acc_ref[h, :, cols] + _dot(vt_h, p_t)

    def one_block(slot, block):
        for h in range(SCORE_LEAD):
            score_block(1 - slot, h, block + 1)
        for h in range(N_HEADS):
            update(h, everyone, s_ref[slot, h], smax_ref[slot, h], bias_ref[h, pl.ds(block, 1), :],
                   vt_ref[0, block, vt_rows(h), :])
            if h + SCORE_LEAD < N_HEADS:
                score_block(1 - slot, h + SCORE_LEAD, block + 1)

    def two_blocks(j, carry):
        one_block(0, 2 * j)
        one_block(1, 2 * j + 1)
        return carry

    lax.fori_loop(0, first // 2, two_blocks, 0)

    key_pos = lax.broadcasted_iota(jnp.int32, (MOBA_BLOCK, tq), 0)
    allowed = jnp.logical_or(second_half, key_pos <= q_col)
    for h in range(SCORE_LEAD):
        score_last_block(h)
    for h in range(N_HEADS):
        s_t = jnp.where(allowed, s_ref[0, h], NEG_INF)
        bias = jnp.where(second_half, bias_ref[h, pl.ds(first, 1), :], 0.0)
        update(h, everyone, s_t, jnp.max(s_t, axis=0, keepdims=True), bias, vt_ref[0, first, vt_rows(h), :])
        if h + SCORE_LEAD < N_HEADS:
            score_last_block(h + SCORE_LEAD)

    causal = (lax.broadcasted_iota(jnp.int32, (MOBA_BLOCK, MOBA_BLOCK), 0)
              <= lax.broadcasted_iota(jnp.int32, (MOBA_BLOCK, MOBA_BLOCK), 1))
    zero_bias = jnp.zeros((1, MOBA_BLOCK), F32)
    for h in range(N_HEADS):
        s_t = jnp.where(causal, s_ref[1, h, :, late], NEG_INF)
        update(h, late, s_t, jnp.max(s_t, axis=0, keepdims=True), zero_bias, vt_ref[0, last, vt_rows(h), :])
        acc = acc_ref[h]
        o_ref[0, h * HEAD_DIM:(h + 1) * HEAD_DIM, :] = (acc[0:HEAD_DIM] / acc[HEAD_DIM:HEAD_DIM + 1]).astype(BF16)


def _attn_call(q, k4, vt, kmean):
    b, s, _ = q.shape
    n_blk = s // MOBA_BLOCK
    tq = Q_BLOCKS * MOBA_BLOCK
    return pl.pallas_call(
        _attn_kernel,
        grid=(b, s // tq),
        in_specs=[
            pl.BlockSpec((1, tq, ATTN_WIDTH), lambda bi, i: (bi, i, 0)),
            pl.BlockSpec((1, n_blk, MOBA_BLOCK, ATTN_WIDTH), lambda bi, i: (bi, 0, 0, 0)),
            pl.BlockSpec((1, n_blk, N_HEADS * VT_ROWS, MOBA_BLOCK), lambda bi, i: (bi, 0, 0, 0)),
            pl.BlockSpec((1, n_blk, ATTN_WIDTH), lambda bi, i: (bi, 0, 0)),
        ],
        out_specs=pl.BlockSpec((1, ATTN_WIDTH, tq), lambda bi, i: (bi, 0, i)),
        out_shape=jax.ShapeDtypeStruct((b, ATTN_WIDTH, s), BF16),
        scratch_shapes=[
            pltpu.VMEM((N_HEADS, n_blk, tq), F32),
            pltpu.VMEM((N_HEADS, LANES, tq), BF16),
            pltpu.VMEM((2, N_HEADS, MOBA_BLOCK, tq), F32),
            pltpu.VMEM((2, N_HEADS, 1, tq), F32),
            pltpu.VMEM((N_HEADS, 1, tq), F32),
            pltpu.VMEM((N_HEADS, VT_ROWS, tq), F32),
        ],
        compiler_params=_params("parallel", "arbitrary"),
        name="moba_attn",
    )(q, k4, vt, kmean)


def _mix_kernel(alpha, x_ref, sc_ref, sh_ref, gt_ref, at_ref, ch_ref, wga0_ref, wga1_ref, wgc0_ref, wgc1_ref,
                wa_ref, wc_ref, wo_ref, lng_ref, lnb_ref, wu_ref, wd_ref, o_ref, wub_ref, wdb_ref):
    wub_ref[...] = wu_ref[...].astype(BF16)
    wdb_ref[...] = wd_ref[...].astype(BF16)

    x = x_ref[0]
    u = x * (1.0 + sc_ref[0]) + sh_ref[0]
    strips = []
    for i, (wga_ref, wgc_ref) in enumerate(((wga0_ref, wgc0_ref), (wga1_ref, wgc1_ref))):
        for j in range(W_IN_BLOCK // MIX_STRIP):
            loc = slice(j * MIX_STRIP, (j + 1) * MIX_STRIP)
            cols = slice(i * W_IN_BLOCK + j * MIX_STRIP, i * W_IN_BLOCK + (j + 1) * MIX_STRIP)
            strips.append(jax.nn.sigmoid(_dot(u, wga_ref[:, loc])) * _dot_tn(at_ref[0], wa_ref[:, cols])
                          + jax.nn.sigmoid(_dot(u, wgc_ref[:, loc])) * _dot(ch_ref[0], wc_ref[:, cols]))
    y = _dot(jnp.concatenate(strips, axis=1), wo_ref[...])
    o_ref[0] = _layer_norm(alpha * x + (1.0 + gt_ref[0]) * y, lng_ref[...], lnb_ref[...])


def _mix_call(layer, alpha, x, ada, attn_t, conv_h, w_in, w_attn_br, w_conv_pw, w_out, ln_g, ln_b, w_up, w_down):
    b, s, d = x.shape
    tm = ROW_TILE
    n_s = s // tm
    n_steps = b * n_s
    d_ff = w_up.shape[2]
    assert d % (n_steps * BF16_ROWS) == 0 and d_ff % (n_steps * BF16_ROWS) == 0
    up_rows, down_rows = d // n_steps, d_ff // n_steps

    def ada_map(slot):
        return lambda bi, si: (_ada_row(layer, bi, slot), 0, 0)

    def whole(arr):
        return pl.BlockSpec((None,) + arr.shape[1:], lambda bi, si: (layer,) + (0,) * (arr.ndim - 1),
                            pipeline_mode=pl.Buffered(1))

    return pl.pallas_call(
        functools.partial(_mix_kernel, alpha),
        grid=(b, s // tm),
        in_specs=[
            pl.BlockSpec((1, tm, d), lambda bi, si: (bi, si, 0)),
            pl.BlockSpec((1, 1, d), ada_map(1)),
            pl.BlockSpec((1, 1, d), ada_map(0)),
            pl.BlockSpec((1, 1, d), ada_map(2)),
            pl.BlockSpec((1, ATTN_WIDTH, tm), lambda bi, si: (bi, 0, si)),
            pl.BlockSpec((1, tm, CONV_WIDTH), lambda bi, si: (bi, si, 0)),
            *[_w_in_block(layer, j) for j in range(5, 9)],
            whole(w_attn_br), whole(w_conv_pw), whole(w_out), whole(ln_g), whole(ln_b),
            pl.BlockSpec((None, up_rows, d_ff), lambda bi, si: (layer, bi * n_s + si, 0)),
            pl.BlockSpec((None, down_rows, d), lambda bi, si: (layer, bi * n_s + si, 0)),
        ],
        out_specs=[
            pl.BlockSpec((1, tm, d), lambda bi, si: (bi, si, 0)),
            pl.BlockSpec((up_rows, d_ff), lambda bi, si: (bi * n_s + si, 0)),
            pl.BlockSpec((down_rows, d), lambda bi, si: (bi * n_s + si, 0)),
        ],
        out_shape=[
            jax.ShapeDtypeStruct((b, s, d), F32),
            jax.ShapeDtypeStruct((d, d_ff), BF16),
            jax.ShapeDtypeStruct((d_ff, d), BF16),
        ],
        compiler_params=_params("parallel", "parallel"),
        name="mix_out",
    )(x, ada, ada, ada, attn_t, conv_h, w_in, w_in, w_in, w_in, w_attn_br, w_conv_pw, w_out, ln_g, ln_b,
      w_up, w_down)


def _ffn_kernel(alpha, x_ref, sc_ref, sh_ref, gt_ref, wu_ref, wd_ref, lng_ref, lnb_ref, o_ref):
    d = D_MODEL
    x = x_ref[0]
    u = (x * (1.0 + sc_ref[0]) + sh_ref[0]).astype(BF16)
    y = None
    for c in range(D_FF // d):
        h = jnp.maximum(_dot(u, wu_ref[:, c * d:(c + 1) * d]), 0.0)
        part = _dot((h * h).astype(BF16), wd_ref[c * d:(c + 1) * d, :])
        y = part if y is None else y + part
    o_ref[0] = _layer_norm(alpha * x + (1.0 + gt_ref[0]) * y, lng_ref[...], lnb_ref[...])


def _ffn_call(layer, alpha, x, ada, w_up_b, w_down_b, ln_g, ln_b):
    b, s, d = x.shape
    tm = ROW_TILE

    def ada_map(slot):
        return lambda bi, si: (_ada_row(layer, bi, slot), 0, 0)

    def whole(arr):
        return pl.BlockSpec((None,) + arr.shape[1:], lambda bi, si: (layer,) + (0,) * (arr.ndim - 1),
                            pipeline_mode=pl.Buffered(1))

    def resident(arr):
        return pl.BlockSpec(arr.shape, lambda bi, si: (0,) * arr.ndim, pipeline_mode=pl.Buffered(1))

    return pl.pallas_call(
        functools.partial(_ffn_kernel, alpha),
        grid=(b, s // tm),
        in_specs=[
            pl.BlockSpec((1, tm, d), lambda bi, si: (bi, si, 0)),
            pl.BlockSpec((1, 1, d), ada_map(4)),
            pl.BlockSpec((1, 1, d), ada_map(3)),
            pl.BlockSpec((1, 1, d), ada_map(5)),
            resident(w_up_b), resident(w_down_b), whole(ln_g), whole(ln_b),
        ],
        out_specs=pl.BlockSpec((1, tm, d), lambda bi, si: (bi, si, 0)),
        out_shape=jax.ShapeDtypeStruct((b, s, d), F32),
        compiler_params=_params("parallel", "parallel"),
        name="ffn",
    )(x, ada, ada, ada, w_up_b, w_down_b, ln_g, ln_b)


def _rope_tables(positions):
    inv_freq = ROPE_THETA ** (-jnp.arange(0, ROT_DIM, 2, dtype=F32) / ROT_DIM)
    ang = positions.astype(F32)[..., None] * inv_freq
    cos, sin = jnp.cos(ang), jnp.sin(ang)
    rest = HEAD_DIM - ROT_DIM
    cos_h = jnp.concatenate([cos, cos, jnp.ones(cos.shape[:-1] + (rest,), F32)], -1)
    sin_h = jnp.concatenate([-sin, sin, jnp.zeros(sin.shape[:-1] + (rest,), F32)], -1)
    return jnp.tile(cos_h, (1, 1, HEADS_PER_GROUP)), jnp.tile(sin_h, (1, 1, HEADS_PER_GROUP))


def kernel(x, c, positions, w_in, w_attn_br, conv_dw_w, conv_dw_b, conv_ln_g, conv_ln_b, w_conv_pw, w_out,
           w_ada, b_ada, ln_mix_g, ln_mix_b, w_up, w_down, ln_ffn_g, ln_ffn_b):
    depth = w_in.shape[0]
    b, s, d = x.shape
    assert d == D_MODEL and s % ROW_TILE == 0 and ROW_TILE % MOBA_BLOCK == 0 and b <= ADA_ROWS
    assert s % (Q_BLOCKS * MOBA_BLOCK) == 0 and s % INPROJ_TILE == 0 and INPROJ_TILE % MOBA_BLOCK == 0
    alpha = (2 * depth) ** 0.25
    assert w_in.shape[2] == 9 * W_IN_BLOCK and ATTN_WIDTH == CONV_WIDTH == W_IN_BLOCK and d == 2 * W_IN_BLOCK
    n_blk = s // MOBA_BLOCK

    c_pad = jnp.zeros((ADA_ROWS, d), F32).at[:b].set(c)
    ada = _ada_call(c_pad, w_ada, b_ada).reshape(depth * ADA_ROWS * ADA_SLOTS, 1, d)
    cosf, sinf = _rope_tables(positions)

    w_attn_br_b = w_attn_br.astype(BF16)
    w_conv_pw_b = w_conv_pw.astype(BF16)
    row = lambda t: t.reshape(depth, 1, t.shape[-1])
    dw_w_tiles = jnp.broadcast_to(conv_dw_w[:, :, None, :], (depth, CONV_KERNEL, SUBLANES, CONV_WIDTH))

    for layer in range(depth):
        q, k, vt, kmean, conv_h = _inproj_call(layer, x, ada, w_in, cosf, sinf, dw_w_tiles,
                                               row(conv_dw_b), row(conv_ln_g), row(conv_ln_b))
        attn_t = _attn_call(q, k.reshape(b, n_blk, MOBA_BLOCK, ATTN_WIDTH), vt,
                            kmean.reshape(b, n_blk, ATTN_WIDTH))
        x, w_up_b, w_down_b = _mix_call(layer, alpha, x, ada, attn_t, conv_h, w_in, w_attn_br_b, w_conv_pw_b, w_out,
                                        row(ln_mix_g), row(ln_mix_b), w_up, w_down)
        x = _ffn_call(layer, alpha, x, ada, w_up_b, w_down_b, row(ln_ffn_g), row(ln_ffn_b))
    return x
```

```python
import functools

import jax
import jax.numpy as jnp
from jax import lax
from jax.experimental import pallas as pl
from jax.experimental.pallas import tpu as pltpu

D_MODEL = 1024
N_HEADS = 8
HEAD_DIM = 64
ATTN_WIDTH = N_HEADS * HEAD_DIM
ROT_DIM = HEAD_DIM // 4
ROPE_THETA = 500000.0
MOBA_BLOCK = 256
MOBA_TOPK = 3
CONV_WIDTH = D_MODEL // 2
CONV_KERNEL = 31
D_FF = 4 * D_MODEL
LN_EPS = 1e-5
NEG_INF = -1e30

LANES = 128
HEADS_PER_GROUP = LANES // HEAD_DIM
SUBLANES = 8
BF16_ROWS = 16
CONV_HALO = 32
CONV_ROWS = 32
VT_ROWS = HEAD_DIM + BF16_ROWS
Q_BLOCKS = 2
SCORE_LEAD = 1
QK_SCALE = HEAD_DIM ** -0.5 * 1.4426950408889634
ROW_TILE = 1024
INPROJ_TILE = 1024
VMEM_LIMIT = 56 * 1024 * 1024

F32 = jnp.float32
BF16 = jnp.bfloat16


def _dot(a, b):
    return jnp.dot(a, b, preferred_element_type=F32)


def _layer_norm(r, g, b):
    mu = jnp.mean(r, axis=-1, keepdims=True)
    d = r - mu
    var = jnp.mean(d * d, axis=-1, keepdims=True)
    return d * lax.rsqrt(var + LN_EPS) * g + b


def _params(*semantics):
    return pltpu.CompilerParams(dimension_semantics=semantics, vmem_limit_bytes=VMEM_LIMIT)


def _ada_kernel(c_ref, w_ref, b_ref, o_ref):
    c = c_ref[...]
    c_act = c * jax.nn.sigmoid(c)
    o_ref[0] = _dot(c_act.astype(BF16), w_ref[0].astype(BF16)) + b_ref[0]


def _ada_call(c_pad, w_ada, b_ada):
    depth, d, six_d = w_ada.shape
    rows = c_pad.shape[0]
    n_chunks = six_d // d
    return pl.pallas_call(
        _ada_kernel,
        grid=(depth, n_chunks),
        in_specs=[
            pl.BlockSpec((rows, d), lambda l, j: (0, 0)),
            pl.BlockSpec((1, d, d), lambda l, j: (l, 0, j)),
            pl.BlockSpec((1, 1, d), lambda l, j: (l, 0, j)),
        ],
        out_specs=pl.BlockSpec((1, rows, d), lambda l, j: (l, 0, j)),
        out_shape=jax.ShapeDtypeStruct((depth, rows, six_d), F32),
        compiler_params=_params("parallel", "parallel"),
        name="ada",
    )(c_pad, w_ada, b_ada.reshape(depth, 1, six_d))


MIX_STRIP = 256
W_IN_BLOCK = 512


def _w_in_block(layer, j):
    return pl.BlockSpec((None, D_MODEL, W_IN_BLOCK), lambda bi, si: (layer, 0, j), pipeline_mode=pl.Buffered(1))


def _inproj_kernel(x_ref, sc_ref, sh_ref, wq_ref, wk_ref, wv_ref, wa_ref, wg_ref, cos_ref, sin_ref,
                   dww_ref, dwb_ref, lng_ref, lnb_ref,
                   q_ref, k_ref, vt_ref, km_ref, ch_ref, hext_ref, hsh_ref):
    tm = x_ref.shape[1]
    a_w = ATTN_WIDTH
    c_w = CONV_WIDTH

    @pl.when(pl.program_id(1) == 0)
    def _():
        hext_ref[0:CONV_HALO, :] = jnp.zeros((CONV_HALO, c_w), F32)

    u = x_ref[0] * (1.0 + sc_ref[0]) + sh_ref[0]
    cosf = cos_ref[0]
    sinf = sin_ref[0]
    lane = lax.broadcasted_iota(jnp.int32, (1, LANES), 1)
    first_half = (lane % HEAD_DIM) < (ROT_DIM // 2)

    def rope(t, c):
        xc = t[:, c * LANES:(c + 1) * LANES]
        partner = jnp.where(first_half,
                            pltpu.roll(xc, LANES - ROT_DIM // 2, 1),
                            pltpu.roll(xc, ROT_DIM // 2, 1))
        return xc * cosf + partner * sinf

    hext_ref[CONV_HALO:CONV_HALO + tm, :] = _dot(u, wa_ref[...]) * jax.nn.sigmoid(_dot(u, wg_ref[...]))

    q = _dot(u, wq_ref[...])
    for c in range(a_w // LANES):
        q_ref[0, :, c * LANES:(c + 1) * LANES] = (rope(q, c) * QK_SCALE).astype(BF16)

    k = _dot(u, wk_ref[...])
    for c in range(a_w // LANES):
        kc = rope(k, c)
        k_ref[0, :, c * LANES:(c + 1) * LANES] = kc.astype(BF16)
        for blk in range(tm // MOBA_BLOCK):
            km_ref[blk, :, c * LANES:(c + 1) * LANES] = jnp.mean(
                kc[blk * MOBA_BLOCK:(blk + 1) * MOBA_BLOCK], axis=0, keepdims=True)

    v = _dot(u, wv_ref[...])
    ones_row = (lax.broadcasted_iota(jnp.int32, (BF16_ROWS, MOBA_BLOCK), 0) == 0).astype(BF16)
    for blk in range(tm // MOBA_BLOCK):
        v_t = v[blk * MOBA_BLOCK:(blk + 1) * MOBA_BLOCK, :].T.astype(BF16)
        for h in range(N_HEADS):
            vt_ref[0, blk, h * VT_ROWS:h * VT_ROWS + HEAD_DIM, :] = v_t[h * HEAD_DIM:(h + 1) * HEAD_DIM, :]
            vt_ref[0, blk, h * VT_ROWS + HEAD_DIM:(h + 1) * VT_ROWS, :] = ones_row

    base = CONV_HALO - (CONV_KERNEL - 1)
    sh_rows = hsh_ref.shape[1]
    for s in range(1, SUBLANES):
        hsh_ref[s - 1] = hext_ref[s:s + sh_rows, :]
    lng = lng_ref[...]
    lnb = lnb_ref[...]
    dwb = dwb_ref[...]
    groups = CONV_ROWS // SUBLANES
    for r in range(0, tm, CONV_ROWS):
        acc = jnp.broadcast_to(dwb.reshape(1, 1, c_w), (groups, SUBLANES, c_w))
        for j in range(CONV_KERNEL):
            s, row = (base + j) % SUBLANES, r + (base + j) // SUBLANES * SUBLANES
            tap = hext_ref[row:row + CONV_ROWS, :] if s == 0 else hsh_ref[s - 1, row:row + CONV_ROWS, :]
            acc = acc + dww_ref[j][None] * tap.reshape(groups, SUBLANES, c_w)
        y = _layer_norm(acc.reshape(CONV_ROWS, c_w), lng, lnb)
        ch_ref[0, r:r + CONV_ROWS, :] = (y * jax.nn.sigmoid(y)).astype(BF16)

    hext_ref[0:CONV_HALO, :] = hext_ref[tm:tm + CONV_HALO, :]


def _inproj_call(layer, x, ada, w_in, cosf, sinf, dw_w, dw_b, cln_g, cln_b):
    b, s, d = x.shape
    tm = INPROJ_TILE
    n_s = s // tm
    blocks_per_tile = tm // MOBA_BLOCK
    n_blk = s // MOBA_BLOCK
    c_w = CONV_WIDTH

    def ada_map(slot):
        return lambda bi, si: (_ada_row(layer, bi, slot), 0, 0)

    in_specs = [
        pl.BlockSpec((1, tm, d), lambda bi, si: (bi, si, 0)),
        pl.BlockSpec((1, 1, d), ada_map(1)),
        pl.BlockSpec((1, 1, d), ada_map(0)),
        *[_w_in_block(layer, j) for j in range(5)],
        pl.BlockSpec((1, tm, LANES), lambda bi, si: (bi, si, 0)),
        pl.BlockSpec((1, tm, LANES), lambda bi, si: (bi, si, 0)),
        pl.BlockSpec((None, CONV_KERNEL, SUBLANES, c_w), lambda bi, si: (layer, 0, 0, 0)),
        pl.BlockSpec((None, 1, c_w), lambda bi, si: (layer, 0, 0)),
        pl.BlockSpec((None, 1, c_w), lambda bi, si: (layer, 0, 0)),
        pl.BlockSpec((None, 1, c_w), lambda bi, si: (layer, 0, 0)),
    ]
    out_specs = [
        pl.BlockSpec((1, tm, ATTN_WIDTH), lambda bi, si: (bi, si, 0)),
        pl.BlockSpec((1, tm, ATTN_WIDTH), lambda bi, si: (bi, si, 0)),
        pl.BlockSpec((1, blocks_per_tile, N_HEADS * VT_ROWS, MOBA_BLOCK), lambda bi, si: (bi, si, 0, 0)),
        pl.BlockSpec((blocks_per_tile, 1, ATTN_WIDTH), lambda bi, si: (bi * n_s + si, 0, 0)),
        pl.BlockSpec((1, tm, c_w), lambda bi, si: (bi, si, 0)),
    ]
    out_shape = [
        jax.ShapeDtypeStruct((b, s, ATTN_WIDTH), BF16),
        jax.ShapeDtypeStruct((b, s, ATTN_WIDTH), BF16),
        jax.ShapeDtypeStruct((b, n_blk, N_HEADS * VT_ROWS, MOBA_BLOCK), BF16),
        jax.ShapeDtypeStruct((b * n_blk, 1, ATTN_WIDTH), F32),
        jax.ShapeDtypeStruct((b, s, c_w), BF16),
    ]
    return pl.pallas_call(
        _inproj_kernel,
        grid=(b, n_s),
        in_specs=in_specs,
        out_specs=out_specs,
        out_shape=out_shape,
        scratch_shapes=[
            pltpu.VMEM((tm + CONV_HALO, c_w), F32),
            pltpu.VMEM((SUBLANES - 1, tm + CONV_HALO - SUBLANES, c_w), F32),
        ],
        compiler_params=_params("parallel", "arbitrary"),
        name="inproj",
    )(x, ada, ada, w_in, w_in, w_in, w_in, w_in, cosf, sinf, dw_w, dw_b, cln_g, cln_b)


ADA_SLOTS = 6
ADA_ROWS = 8


def _ada_row(layer, bi, slot):
    return (layer * ADA_ROWS + bi) * ADA_SLOTS + slot


def _attn_kernel(q_ref, k_ref, vt_ref, km_ref, o_ref, bias_ref, qh_ref, s_ref, smax_ref, m_ref, acc_ref):
    n_blk = k_ref.shape[1]
    tq = q_ref.shape[1]
    first = Q_BLOCKS * pl.program_id(1)
    last = first + 1
    blk_col = lax.broadcasted_iota(jnp.int32, (n_blk, 1), 0)
    q_col = lax.broadcasted_iota(jnp.int32, (1, tq), 1)
    second_half = q_col >= MOBA_BLOCK
    valid = blk_col < first + second_half.astype(jnp.int32)
    lane = lax.broadcasted_iota(jnp.int32, (1, LANES), 1)
    late = slice(MOBA_BLOCK, tq)
    everyone = slice(0, tq)

    def group_lanes(h):
        g = h // HEADS_PER_GROUP
        return slice(g * LANES, (g + 1) * LANES)

    def vt_rows(h):
        return slice(h * VT_ROWS, (h + 1) * VT_ROWS)

    def score_block(slot, h, block):
        s_t = _dot(k_ref[0, block, :, group_lanes(h)], qh_ref[h])
        s_ref[slot, h] = s_t
        smax_ref[slot, h] = jnp.max(s_t, axis=0, keepdims=True)

    def score_last_block(h):
        s_ref[1, h, :, late] = _dot(k_ref[0, last, :, group_lanes(h)], qh_ref[h, :, late])

    dim_row = lax.broadcasted_iota(jnp.int32, (LANES, 1), 0)
    for h in range(N_HEADS):
        in_head = (lane // HEAD_DIM) == (h % HEADS_PER_GROUP)
        if h % HEADS_PER_GROUP == 0:
            q_gt = q_ref[0, :, group_lanes(h)].astype(F32).T
        q_h = jnp.where((dim_row // HEAD_DIM) == (h % HEADS_PER_GROUP), q_gt, 0.0).astype(BF16)
        qh_ref[h] = q_h
        km_h = jnp.where(in_head, km_ref[0, :, group_lanes(h)], 0.0)
        km_hi = km_h.astype(BF16)
        km_lo = (km_h - km_hi.astype(F32)).astype(BF16)
        gate = _dot(km_hi, q_h) + _dot(km_lo, q_h)
        gate = jnp.where(valid, gate, NEG_INF)
        rank = jnp.zeros((n_blk, tq), F32)
        for n2 in range(n_blk):
            row = gate[n2:n2 + 1, :]
            ahead = jnp.where(row > gate, 1.0, jnp.where(row == gate, (blk_col > n2).astype(F32), 0.0))
            rank = rank + ahead
        selected = jnp.where(rank < MOBA_TOPK, valid.astype(F32), 0.0)
        bias_ref[h] = jnp.where(selected > 0.0, 0.0, -NEG_INF)
        m_ref[h] = jnp.full((1, tq), NEG_INF, F32)
        acc_ref[h] = jnp.zeros((VT_ROWS, tq), F32)
    for h in range(N_HEADS):
        score_block(0, h, 0)

    def update(h, cols, s_t, s_max, bias, vt_h):
        m = m_ref[h, :, cols]
        m_new = jnp.maximum(m, s_max - bias)
        alpha = jnp.exp2(m - m_new)
        p_t = jnp.exp2(s_t - (m_new + bias)).astype(BF16)
        m_ref[h, :, cols] = m_new
        acc_ref[h, :, cols] = alpha * acc_ref[h, :, cols] + _dot(vt_h, p_t)

    def one_block(slot, block):
        for h in range(SCORE_LEAD):
            score_block(1 - slot, h, block + 1)
        for h in range(N_HEADS):
            update(h, everyone, s_ref[slot, h], smax_ref[slot, h], bias_ref[h, pl.ds(block, 1), :],
                   vt_ref[0, block, vt_rows(h), :])
            if h + SCORE_LEAD < N_HEADS:
                score_block(1 - slot, h + SCORE_LEAD, block + 1)

    def two_blocks(j, carry):
        one_block(0, 2 * j)
        one_block(1, 2 * j + 1)
        return carry

    lax.fori_loop(0, first // 2, two_blocks, 0)

    key_pos = lax.broadcasted_iota(jnp.int32, (MOBA_BLOCK, tq), 0)
    allowed = jnp.logical_or(second_half, key_pos <= q_col)
    for h in range(SCORE_LEAD):
        score_last_block(h)
    for h in range(N_HEADS):
        s_t = jnp.where(allowed, s_ref[0, h], NEG_INF)
        bias = jnp.where(second_half, bias_ref[h, pl.ds(first, 1), :], 0.0)
        update(h, everyone, s_t, jnp.max(s_t, axis=0, keepdims=True), bias, vt_ref[0, first, vt_rows(h), :])
        if h + SCORE_LEAD < N_HEADS:
            score_last_block(h + SCORE_LEAD)

    causal = (lax.broadcasted_iota(jnp.int32, (MOBA_BLOCK, MOBA_BLOCK), 0)
              <= lax.broadcasted_iota(jnp.int32, (MOBA_BLOCK, MOBA_BLOCK), 1))
    zero_bias = jnp.zeros((1, MOBA_BLOCK), F32)
    for h in range(N_HEADS):
        s_t = jnp.where(causal, s_ref[1, h, :, late], NEG_INF)
        update(h, late, s_t, jnp.max(s_t, axis=0, keepdims=True), zero_bias, vt_ref[0, last, vt_rows(h), :])
        acc = acc_ref[h]
        out_t = acc[0:HEAD_DIM] / acc[HEAD_DIM:HEAD_DIM + 1]
        if h % HEADS_PER_GROUP == 0:
            group_t = [out_t]
        else:
            group_t.append(out_t)
        if len(group_t) == HEADS_PER_GROUP:
            o_ref[0, :, group_lanes(h)] = jnp.concatenate(group_t, axis=0).T.astype(BF16)


def _attn_call(q, k4, vt, kmean):
    b, s, _ = q.shape
    n_blk = s // MOBA_BLOCK
    tq = Q_BLOCKS * MOBA_BLOCK
    return pl.pallas_call(
        _attn_kernel,
        grid=(b, s // tq),
        in_specs=[
            pl.BlockSpec((1, tq, ATTN_WIDTH), lambda bi, i: (bi, i, 0)),
            pl.BlockSpec((1, n_blk, MOBA_BLOCK, ATTN_WIDTH), lambda bi, i: (bi, 0, 0, 0)),
            pl.BlockSpec((1, n_blk, N_HEADS * VT_ROWS, MOBA_BLOCK), lambda bi, i: (bi, 0, 0, 0)),
            pl.BlockSpec((1, n_blk, ATTN_WIDTH), lambda bi, i: (bi, 0, 0)),
        ],
        out_specs=pl.BlockSpec((1, tq, ATTN_WIDTH), lambda bi, i: (bi, i, 0)),
        out_shape=jax.ShapeDtypeStruct((b, s, ATTN_WIDTH), BF16),
        scratch_shapes=[
            pltpu.VMEM((N_HEADS, n_blk, tq), F32),
            pltpu.VMEM((N_HEADS, LANES, tq), BF16),
            pltpu.VMEM((2, N_HEADS, MOBA_BLOCK, tq), F32),
            pltpu.VMEM((2, N_HEADS, 1, tq), F32),
            pltpu.VMEM((N_HEADS, 1, tq), F32),
            pltpu.VMEM((N_HEADS, VT_ROWS, tq), F32),
        ],
        compiler_params=_params("parallel", "arbitrary"),
        name="moba_attn",
    )(q, k4, vt, kmean)


def _mix_kernel(alpha, x_ref, sc_ref, sh_ref, gt_ref, at_ref, ch_ref, wga0_ref, wga1_ref, wgc0_ref, wgc1_ref,
                wa_ref, wc_ref, wo_ref, lng_ref, lnb_ref, wu_ref, wd_ref, o_ref, wub_ref, wdb_ref):
    wub_ref[...] = wu_ref[...].astype(BF16)
    wdb_ref[...] = wd_ref[...].astype(BF16)

    x = x_ref[0]
    u = x * (1.0 + sc_ref[0]) + sh_ref[0]
    strips = []
    for i, (wga_ref, wgc_ref) in enumerate(((wga0_ref, wgc0_ref), (wga1_ref, wgc1_ref))):
        for j in range(W_IN_BLOCK // MIX_STRIP):
            loc = slice(j * MIX_STRIP, (j + 1) * MIX_STRIP)
            cols = slice(i * W_IN_BLOCK + j * MIX_STRIP, i * W_IN_BLOCK + (j + 1) * MIX_STRIP)
            strips.append(jax.nn.sigmoid(_dot(u, wga_ref[:, loc])) * _dot(at_ref[0], wa_ref[:, cols])
                          + jax.nn.sigmoid(_dot(u, wgc_ref[:, loc])) * _dot(ch_ref[0], wc_ref[:, cols]))
    y = _dot(jnp.concatenate(strips, axis=1), wo_ref[...])
    o_ref[0] = _layer_norm(alpha * x + (1.0 + gt_ref[0]) * y, lng_ref[...], lnb_ref[...])


def _mix_call(layer, alpha, x, ada, attn_t, conv_h, w_in, w_attn_br, w_conv_pw, w_out, ln_g, ln_b, w_up, w_down):
    b, s, d = x.shape
    tm = ROW_TILE
    n_s = s // tm
    n_steps = b * n_s
    d_ff = w_up.shape[2]
    assert d % (n_steps * BF16_ROWS) == 0 and d_ff % (n_steps * BF16_ROWS) == 0
    up_rows, down_rows = d // n_steps, d_ff // n_steps

    def ada_map(slot):
        return lambda bi, si: (_ada_row(layer, bi, slot), 0, 0)

    def whole(arr):
        return pl.BlockSpec((None,) + arr.shape[1:], lambda bi, si: (layer,) + (0,) * (arr.ndim - 1),
                            pipeline_mode=pl.Buffered(1))

    return pl.pallas_call(
        functools.partial(_mix_kernel, alpha),
        grid=(b, s // tm),
        in_specs=[
            pl.BlockSpec((1, tm, d), lambda bi, si: (bi, si, 0)),
            pl.BlockSpec((1, 1, d), ada_map(1)),
            pl.BlockSpec((1, 1, d), ada_map(0)),
            pl.BlockSpec((1, 1, d), ada_map(2)),
            pl.BlockSpec((1, tm, ATTN_WIDTH), lambda bi, si: (bi, si, 0)),
            pl.BlockSpec((1, tm, CONV_WIDTH), lambda bi, si: (bi, si, 0)),
            *[_w_in_block(layer, j) for j in range(5, 9)],
            whole(w_attn_br), whole(w_conv_pw), whole(w_out), whole(ln_g), whole(ln_b),
            pl.BlockSpec((None, up_rows, d_ff), lambda bi, si: (layer, bi * n_s + si, 0)),
            pl.BlockSpec((None, down_rows, d), lambda bi, si: (layer, bi * n_s + si, 0)),
        ],
        out_specs=[
            pl.BlockSpec((1, tm, d), lambda bi, si: (bi, si, 0)),
            pl.BlockSpec((up_rows, d_ff), lambda bi, si: (bi * n_s + si, 0)),
            pl.BlockSpec((down_rows, d), lambda bi, si: (bi * n_s + si, 0)),
        ],
        out_shape=[
            jax.ShapeDtypeStruct((b, s, d), F32),
            jax.ShapeDtypeStruct((d, d_ff), BF16),
            jax.ShapeDtypeStruct((d_ff, d), BF16),
        ],
        compiler_params=_params("parallel", "parallel"),
        name="mix_out",
    )(x, ada, ada, ada, attn_t, conv_h, w_in, w_in, w_in, w_in, w_attn_br, w_conv_pw, w_out, ln_g, ln_b,
      w_up, w_down)


def _ffn_kernel(alpha, x_ref, sc_ref, sh_ref, gt_ref, wu_ref, wd_ref, lng_ref, lnb_ref, o_ref):
    d = D_MODEL
    x = x_ref[0]
    u = (x * (1.0 + sc_ref[0]) + sh_ref[0]).astype(BF16)
    y = None
    for c in range(D_FF // d):
        h = jnp.maximum(_dot(u, wu_ref[:, c * d:(c + 1) * d]), 0.0)
        part = _dot((h * h).astype(BF16), wd_ref[c * d:(c + 1) * d, :])
        y = part if y is None else y + part
    o_ref[0] = _layer_norm(alpha * x + (1.0 + gt_ref[0]) * y, lng_ref[...], lnb_ref[...])


def _ffn_call(layer, alpha, x, ada, w_up_b, w_down_b, ln_g, ln_b):
    b, s, d = x.shape
    tm = ROW_TILE

    def ada_map(slot):
        return lambda bi, si: (_ada_row(layer, bi, slot), 0, 0)

    def whole(arr):
        return pl.BlockSpec((None,) + arr.shape[1:], lambda bi, si: (layer,) + (0,) * (arr.ndim - 1),
                            pipeline_mode=pl.Buffered(1))

    def resident(arr):
        return pl.BlockSpec(arr.shape, lambda bi, si: (0,) * arr.ndim, pipeline_mode=pl.Buffered(1))

    return pl.pallas_call(
        functools.partial(_ffn_kernel, alpha),
        grid=(b, s // tm),
        in_specs=[
            pl.BlockSpec((1, tm, d), lambda bi, si: (bi, si, 0)),
            pl.BlockSpec((1, 1, d), ada_map(4)),
            pl.BlockSpec((1, 1, d), ada_map(3)),
            pl.BlockSpec((1, 1, d), ada_map(5)),
            resident(w_up_b), resident(w_down_b), whole(ln_g), whole(ln_b),
        ],
        out_specs=pl.BlockSpec((1, tm, d), lambda bi, si: (bi, si, 0)),
        out_shape=jax.ShapeDtypeStruct((b, s, d), F32),
        compiler_params=_params("parallel", "parallel"),
        name="ffn",
    )(x, ada, ada, ada, w_up_b, w_down_b, ln_g, ln_b)


def _rope_tables(positions):
    inv_freq = ROPE_THETA ** (-jnp.arange(0, ROT_DIM, 2, dtype=F32) / ROT_DIM)
    ang = positions.astype(F32)[..., None] * inv_freq
    cos, sin = jnp.cos(ang), jnp.sin(ang)
    rest = HEAD_DIM - ROT_DIM
    cos_h = jnp.concatenate([cos, cos, jnp.ones(cos.shape[:-1] + (rest,), F32)], -1)
    sin_h = jnp.concatenate([-sin, sin, jnp.zeros(sin.shape[:-1] + (rest,), F32)], -1)
    return jnp.tile(cos_h, (1, 1, HEADS_PER_GROUP)), jnp.tile(sin_h, (1, 1, HEADS_PER_GROUP))


def kernel(x, c, positions, w_in, w_attn_br, conv_dw_w, conv_dw_b, conv_ln_g, conv_ln_b, w_conv_pw, w_out,
           w_ada, b_ada, ln_mix_g, ln_mix_b, w_up, w_down, ln_ffn_g, ln_ffn_b):
    depth = w_in.shape[0]
    b, s, d = x.shape
    assert d == D_MODEL and s % ROW_TILE == 0 and ROW_TILE % MOBA_BLOCK == 0 and b <= ADA_ROWS
    assert s % (Q_BLOCKS * MOBA_BLOCK) == 0 and s % INPROJ_TILE == 0 and INPROJ_TILE % MOBA_BLOCK == 0
    alpha = (2 * depth) ** 0.25
    assert w_in.shape[2] == 9 * W_IN_BLOCK and ATTN_WIDTH == CONV_WIDTH == W_IN_BLOCK and d == 2 * W_IN_BLOCK
    n_blk = s // MOBA_BLOCK

    c_pad = jnp.zeros((ADA_ROWS, d), F32).at[:b].set(c)
    ada = _ada_call(c_pad, w_ada, b_ada).reshape(depth * ADA_ROWS * ADA_SLOTS, 1, d)
    cosf, sinf = _rope_tables(positions)

    w_attn_br_b = w_attn_br.astype(BF16)
    w_conv_pw_b = w_conv_pw.astype(BF16)
    row = lambda t: t.reshape(depth, 1, t.shape[-1])
    dw_w_tiles = jnp.broadcast_to(conv_dw_w[:, :, None, :], (depth, CONV_KERNEL, SUBLANES, CONV_WIDTH))

    for layer in range(depth):
        q, k, vt, kmean, conv_h = _inproj_call(layer, x, ada, w_in, cosf, sinf, dw_w_tiles,
                                               row(conv_dw_b), row(conv_ln_g), row(conv_ln_b))
        attn_t = _attn_call(q, k.reshape(b, n_blk, MOBA_BLOCK, ATTN_WIDTH), vt,
                            kmean.reshape(b, n_blk, ATTN_WIDTH))
        x, w_up_b, w_down_b = _mix_call(layer, alpha, x, ada, attn_t, conv_h, w_in, w_attn_br_b, w_conv_pw_b, w_out,
                                        row(ln_mix_g), row(ln_mix_b), w_up, w_down)
        x = _ffn_call(layer, alpha, x, ada, w_up_b, w_down_b, row(ln_ffn_g), row(ln_ffn_b))
    return x
```

```python
import functools

import jax
import jax.numpy as jnp
from jax import lax
from jax.experimental import pallas as pl
from jax.experimental.pallas import tpu as pltpu

D_MODEL = 1024
N_HEADS = 8
HEAD_DIM = 64
ATTN_WIDTH = N_HEADS * HEAD_DIM
ROT_DIM = HEAD_DIM // 4
ROPE_THETA = 500000.0
MOBA_BLOCK = 256
MOBA_TOPK = 3
CONV_WIDTH = D_MODEL // 2
CONV_KERNEL = 31
D_FF = 4 * D_MODEL
LN_EPS = 1e-5
NEG_INF = -1e30

LANES = 128
HEADS_PER_GROUP = LANES // HEAD_DIM
SUBLANES = 8
BF16_ROWS = 16
CONV_HALO = 32
CONV_ROWS = 32
VT_ROWS = HEAD_DIM + BF16_ROWS
Q_BLOCKS = 2
SCORE_LEAD = 1
QK_SCALE = HEAD_DIM ** -0.5 * 1.4426950408889634
ROW_TILE = 1024
INPROJ_TILE = 1024
VMEM_LIMIT = 56 * 1024 * 1024

F32 = jnp.float32
BF16 = jnp.bfloat16


def _dot(a, b):
    return jnp.dot(a, b, preferred_element_type=F32)


def _layer_norm(r, g, b):
    mu = jnp.mean(r, axis=-1, keepdims=True)
    d = r - mu
    var = jnp.mean(d * d, axis=-1, keepdims=True)
    return d * lax.rsqrt(var + LN_EPS) * g + b


def _params(*semantics):
    return pltpu.CompilerParams(dimension_semantics=semantics, vmem_limit_bytes=VMEM_LIMIT)


def _ada_kernel(c_ref, w_ref, b_ref, o_ref):
    c = c_ref[...]
    c_act = c * jax.nn.sigmoid(c)
    o_ref[0] = _dot(c_act, w_ref[0]) + b_ref[0]


def _ada_call(c_pad, w_ada, b_ada):
    depth, d, six_d = w_ada.shape
    rows = c_pad.shape[0]
    n_chunks = 2
    cw = six_d // n_chunks
    return pl.pallas_call(
        _ada_kernel,
        grid=(depth, n_chunks),
        in_specs=[
            pl.BlockSpec((rows, d), lambda l, j: (0, 0)),
            pl.BlockSpec((1, d, cw), lambda l, j: (l, 0, j)),
            pl.BlockSpec((1, 1, cw), lambda l, j: (l, 0, j)),
        ],
        out_specs=pl.BlockSpec((1, rows, cw), lambda l, j: (l, 0, j)),
        out_shape=jax.ShapeDtypeStruct((depth, rows, six_d), F32),
        compiler_params=_params("parallel", "parallel"),
        name="ada",
    )(c_pad, w_ada, b_ada.reshape(depth, 1, six_d))


MIX_STRIP = 256
W_IN_BLOCK = 512


def _w_in_block(layer, j):
    return pl.BlockSpec((None, D_MODEL, W_IN_BLOCK), lambda bi, si: (layer, 0, j), pipeline_mode=pl.Buffered(1))


def _inproj_kernel(x_ref, sc_ref, sh_ref, wq_ref, wk_ref, wv_ref, wa_ref, wg_ref, cos_ref, sin_ref,
                   dww_ref, dwb_ref, lng_ref, lnb_ref,
                   q_ref, k_ref, vt_ref, km_ref, ch_ref, hext_ref, hsh_ref):
    tm = x_ref.shape[1]
    a_w = ATTN_WIDTH
    c_w = CONV_WIDTH

    @pl.when(pl.program_id(1) == 0)
    def _():
        hext_ref[0:CONV_HALO, :] = jnp.zeros((CONV_HALO, c_w), F32)

    u = x_ref[0] * (1.0 + sc_ref[0]) + sh_ref[0]
    cosf = cos_ref[0]
    sinf = sin_ref[0]
    lane = lax.broadcasted_iota(jnp.int32, (1, LANES), 1)
    first_half = (lane % HEAD_DIM) < (ROT_DIM // 2)

    def rope(t, c):
        xc = t[:, c * LANES:(c + 1) * LANES]
        partner = jnp.where(first_half,
                            pltpu.roll(xc, LANES - ROT_DIM // 2, 1),
                            pltpu.roll(xc, ROT_DIM // 2, 1))
        return xc * cosf + partner * sinf

    hext_ref[CONV_HALO:CONV_HALO + tm, :] = _dot(u, wa_ref[...]) * jax.nn.sigmoid(_dot(u, wg_ref[...]))

    q = _dot(u, wq_ref[...])
    for c in range(a_w // LANES):
        q_ref[0, :, c * LANES:(c + 1) * LANES] = (rope(q, c) * QK_SCALE).astype(BF16)

    k = _dot(u, wk_ref[...])
    for c in range(a_w // LANES):
        kc = rope(k, c)
        k_ref[0, :, c * LANES:(c + 1) * LANES] = kc.astype(BF16)
        for blk in range(tm // MOBA_BLOCK):
            km_ref[blk, :, c * LANES:(c + 1) * LANES] = jnp.mean(
                kc[blk * MOBA_BLOCK:(blk + 1) * MOBA_BLOCK], axis=0, keepdims=True)

    v = _dot(u, wv_ref[...])
    ones_row = (lax.broadcasted_iota(jnp.int32, (BF16_ROWS, MOBA_BLOCK), 0) == 0).astype(BF16)
    for blk in range(tm // MOBA_BLOCK):
        v_t = v[blk * MOBA_BLOCK:(blk + 1) * MOBA_BLOCK, :].T.astype(BF16)
        for h in range(N_HEADS):
            vt_ref[0, blk, h * VT_ROWS:h * VT_ROWS + HEAD_DIM, :] = v_t[h * HEAD_DIM:(h + 1) * HEAD_DIM, :]
            vt_ref[0, blk, h * VT_ROWS + HEAD_DIM:(h + 1) * VT_ROWS, :] = ones_row

    base = CONV_HALO - (CONV_KERNEL - 1)
    sh_rows = hsh_ref.shape[1]
    for s in range(1, SUBLANES):
        hsh_ref[s - 1] = hext_ref[s:s + sh_rows, :]
    lng = lng_ref[...]
    lnb = lnb_ref[...]
    dwb = dwb_ref[...]
    groups = CONV_ROWS // SUBLANES
    for r in range(0, tm, CONV_ROWS):
        acc = jnp.broadcast_to(dwb.reshape(1, 1, c_w), (groups, SUBLANES, c_w))
        for j in range(CONV_KERNEL):
            s, row = (base + j) % SUBLANES, r + (base + j) // SUBLANES * SUBLANES
            tap = hext_ref[row:row + CONV_ROWS, :] if s == 0 else hsh_ref[s - 1, row:row + CONV_ROWS, :]
            acc = acc + dww_ref[j][None] * tap.reshape(groups, SUBLANES, c_w)
        y = _layer_norm(acc.reshape(CONV_ROWS, c_w), lng, lnb)
        ch_ref[0, r:r + CONV_ROWS, :] = (y * jax.nn.sigmoid(y)).astype(BF16)

    hext_ref[0:CONV_HALO, :] = hext_ref[tm:tm + CONV_HALO, :]


def _inproj_call(layer, x, ada, w_in, cosf, sinf, dw_w, dw_b, cln_g, cln_b):
    b, s, d = x.shape
    tm = INPROJ_TILE
    n_s = s // tm
    blocks_per_tile = tm // MOBA_BLOCK
    n_blk = s // MOBA_BLOCK
    c_w = CONV_WIDTH

    def ada_map(slot):
        return lambda bi, si: (_ada_row(layer, bi, slot), 0, 0)

    in_specs = [
        pl.BlockSpec((1, tm, d), lambda bi, si: (bi, si, 0)),
        pl.BlockSpec((1, 1, d), ada_map(1)),
        pl.BlockSpec((1, 1, d), ada_map(0)),
        *[_w_in_block(layer, j) for j in range(5)],
        pl.BlockSpec((1, tm, LANES), lambda bi, si: (bi, si, 0)),
        pl.BlockSpec((1, tm, LANES), lambda bi, si: (bi, si, 0)),
        pl.BlockSpec((None, CONV_KERNEL, SUBLANES, c_w), lambda bi, si: (layer, 0, 0, 0)),
        pl.BlockSpec((None, 1, c_w), lambda bi, si: (layer, 0, 0)),
        pl.BlockSpec((None, 1, c_w), lambda bi, si: (layer, 0, 0)),
        pl.BlockSpec((None, 1, c_w), lambda bi, si: (layer, 0, 0)),
    ]
    out_specs = [
        pl.BlockSpec((1, tm, ATTN_WIDTH), lambda bi, si: (bi, si, 0)),
        pl.BlockSpec((1, tm, ATTN_WIDTH), lambda bi, si: (bi, si, 0)),
        pl.BlockSpec((1, blocks_per_tile, N_HEADS * VT_ROWS, MOBA_BLOCK), lambda bi, si: (bi, si, 0, 0)),
        pl.BlockSpec((blocks_per_tile, 1, ATTN_WIDTH), lambda bi, si: (bi * n_s + si, 0, 0)),
        pl.BlockSpec((1, tm, c_w), lambda bi, si: (bi, si, 0)),
    ]
    out_shape = [
        jax.ShapeDtypeStruct((b, s, ATTN_WIDTH), BF16),
        jax.ShapeDtypeStruct((b, s, ATTN_WIDTH), BF16),
        jax.ShapeDtypeStruct((b, n_blk, N_HEADS * VT_ROWS, MOBA_BLOCK), BF16),
        jax.ShapeDtypeStruct((b * n_blk, 1, ATTN_WIDTH), F32),
        jax.ShapeDtypeStruct((b, s, c_w), BF16),
    ]
    return pl.pallas_call(
        _inproj_kernel,
        grid=(b, n_s),
        in_specs=in_specs,
        out_specs=out_specs,
        out_shape=out_shape,
        scratch_shapes=[
            pltpu.VMEM((tm + CONV_HALO, c_w), F32),
            pltpu.VMEM((SUBLANES - 1, tm + CONV_HALO - SUBLANES, c_w), F32),
        ],
        compiler_params=_params("parallel", "arbitrary"),
        name="inproj",
    )(x, ada, ada, w_in, w_in, w_in, w_in, w_in, cosf, sinf, dw_w, dw_b, cln_g, cln_b)


ADA_SLOTS = 6
ADA_ROWS = 8


def _ada_row(layer, bi, slot):
    return (layer * ADA_ROWS + bi) * ADA_SLOTS + slot


def _attn_kernel(q_ref, k_ref, vt_ref, km_ref, o_ref, bias_ref, qh_ref, s_ref, smax_ref, m_ref, acc_ref):
    n_blk = k_ref.shape[1]
    tq = q_ref.shape[1]
    first = Q_BLOCKS * pl.program_id(1)
    last = first + 1
    blk_col = lax.broadcasted_iota(jnp.int32, (n_blk, 1), 0)
    q_col = lax.broadcasted_iota(jnp.int32, (1, tq), 1)
    second_half = q_col >= MOBA_BLOCK
    valid = blk_col < first + second_half.astype(jnp.int32)
    lane = lax.broadcasted_iota(jnp.int32, (1, LANES), 1)
    late = slice(MOBA_BLOCK, tq)
    everyone = slice(0, tq)

    def group_lanes(h):
        g = h // HEADS_PER_GROUP
        return slice(g * LANES, (g + 1) * LANES)

    def vt_rows(h):
        return slice(h * VT_ROWS, (h + 1) * VT_ROWS)

    def score_block(slot, h, block):
        s_t = _dot(k_ref[0, block, :, group_lanes(h)], qh_ref[h])
        s_ref[slot, h] = s_t
        smax_ref[slot, h] = jnp.max(s_t, axis=0, keepdims=True)

    def score_last_block(h):
        s_ref[1, h, :, late] = _dot(k_ref[0, last, :, group_lanes(h)], qh_ref[h, :, late])

    dim_row = lax.broadcasted_iota(jnp.int32, (LANES, 1), 0)
    for h in range(N_HEADS):
        in_head = (lane // HEAD_DIM) == (h % HEADS_PER_GROUP)
        if h % HEADS_PER_GROUP == 0:
            q_gt = q_ref[0, :, group_lanes(h)].astype(F32).T
        q_h = jnp.where((dim_row // HEAD_DIM) == (h % HEADS_PER_GROUP), q_gt, 0.0).astype(BF16)
        qh_ref[h] = q_h
        km_h = jnp.where(in_head, km_ref[0, :, group_lanes(h)], 0.0)
        km_hi = km_h.astype(BF16)
        km_lo = (km_h - km_hi.astype(F32)).astype(BF16)
        gate = _dot(km_hi, q_h) + _dot(km_lo, q_h)
        gate = jnp.where(valid, gate, NEG_INF)
        rank = jnp.zeros((n_blk, tq), F32)
        for n2 in range(n_blk):
            row = gate[n2:n2 + 1, :]
            ahead = jnp.where(row > gate, 1.0, jnp.where(row == gate, (blk_col > n2).astype(F32), 0.0))
            rank = rank + ahead
        selected = jnp.where(rank < MOBA_TOPK, valid.astype(F32), 0.0)
        bias_ref[h] = jnp.where(selected > 0.0, 0.0, -NEG_INF)
        m_ref[h] = jnp.full((1, tq), NEG_INF, F32)
        acc_ref[h] = jnp.zeros((VT_ROWS, tq), F32)
    for h in range(N_HEADS):
        score_block(0, h, 0)

    def update(h, cols, s_t, s_max, bias, vt_h):
        m = m_ref[h, :, cols]
        m_new = jnp.maximum(m, s_max - bias)
        alpha = jnp.exp2(m - m_new)
        p_t = jnp.exp2(s_t - (m_new + bias)).astype(BF16)
        m_ref[h, :, cols] = m_new
        acc_ref[h, :, cols] = alpha * acc_ref[h, :, cols] + _dot(vt_h, p_t)

    def one_block(slot, block):
        for h in range(SCORE_LEAD):
            score_block(1 - slot, h, block + 1)
        for h in range(N_HEADS):
            update(h, everyone, s_ref[slot, h], smax_ref[slot, h], bias_ref[h, pl.ds(block, 1), :],
                   vt_ref[0, block, vt_rows(h), :])
            if h + SCORE_LEAD < N_HEADS:
                score_block(1 - slot, h + SCORE_LEAD, block + 1)

    def two_blocks(j, carry):
        one_block(0, 2 * j)
        one_block(1, 2 * j + 1)
        return carry

    lax.fori_loop(0, first // 2, two_blocks, 0)

    key_pos = lax.broadcasted_iota(jnp.int32, (MOBA_BLOCK, tq), 0)
    allowed = jnp.logical_or(second_half, key_pos <= q_col)
    for h in range(SCORE_LEAD):
        score_last_block(h)
    for h in range(N_HEADS):
        s_t = jnp.where(allowed, s_ref[0, h], NEG_INF)
        bias = jnp.where(second_half, bias_ref[h, pl.ds(first, 1), :], 0.0)
        update(h, everyone, s_t, jnp.max(s_t, axis=0, keepdims=True), bias, vt_ref[0, first, vt_rows(h), :])
        if h + SCORE_LEAD < N_HEADS:
            score_last_block(h + SCORE_LEAD)

    causal = (lax.broadcasted_iota(jnp.int32, (MOBA_BLOCK, MOBA_BLOCK), 0)
              <= lax.broadcasted_iota(jnp.int32, (MOBA_BLOCK, MOBA_BLOCK), 1))
    zero_bias = jnp.zeros((1, MOBA_BLOCK), F32)
    for h in range(N_HEADS):
        s_t = jnp.where(causal, s_ref[1, h, :, late], NEG_INF)
        update(h, late, s_t, jnp.max(s_t, axis=0, keepdims=True), zero_bias, vt_ref[0, last, vt_rows(h), :])
        acc = acc_ref[h]
        out_t = acc[0:HEAD_DIM] / acc[HEAD_DIM:HEAD_DIM + 1]
        if h % HEADS_PER_GROUP == 0:
            group_t = [out_t]
        else:
            group_t.append(out_t)
        if len(group_t) == HEADS_PER_GROUP:
            o_ref[0, :, group_lanes(h)] = jnp.concatenate(group_t, axis=0).T.astype(BF16)


def _attn_call(q, k4, vt, kmean):
    b, s, _ = q.shape
    n_blk = s // MOBA_BLOCK
    tq = Q_BLOCKS * MOBA_BLOCK
    return pl.pallas_call(
        _attn_kernel,
        grid=(b, s // tq),
        in_specs=[
            pl.BlockSpec((1, tq, ATTN_WIDTH), lambda bi, i: (bi, i, 0)),
            pl.BlockSpec((1, n_blk, MOBA_BLOCK, ATTN_WIDTH), lambda bi, i: (bi, 0, 0, 0)),
            pl.BlockSpec((1, n_blk, N_HEADS * VT_ROWS, MOBA_BLOCK), lambda bi, i: (bi, 0, 0, 0)),
            pl.BlockSpec((1, n_blk, ATTN_WIDTH), lambda bi, i: (bi, 0, 0)),
        ],
        out_specs=pl.BlockSpec((1, tq, ATTN_WIDTH), lambda bi, i: (bi, i, 0)),
        out_shape=jax.ShapeDtypeStruct((b, s, ATTN_WIDTH), BF16),
        scratch_shapes=[
            pltpu.VMEM((N_HEADS, n_blk, tq), F32),
            pltpu.VMEM((N_HEADS, LANES, tq), BF16),
            pltpu.VMEM((2, N_HEADS, MOBA_BLOCK, tq), F32),
            pltpu.VMEM((2, N_HEADS, 1, tq), F32),
            pltpu.VMEM((N_HEADS, 1, tq), F32),
            pltpu.VMEM((N_HEADS, VT_ROWS, tq), F32),
        ],
        compiler_params=_params("parallel", "arbitrary"),
        name="moba_attn",
    )(q, k4, vt, kmean)


def _mix_kernel(alpha, x_ref, sc_ref, sh_ref, gt_ref, at_ref, ch_ref, wga0_ref, wga1_ref, wgc0_ref, wgc1_ref,
                wa_ref, wc_ref, wo_ref, lng_ref, lnb_ref, wu_ref, wd_ref, o_ref, wub_ref, wdb_ref):
    wub_ref[...] = wu_ref[...].astype(BF16)
    wdb_ref[...] = wd_ref[...].astype(BF16)

    x = x_ref[0]
    u = x * (1.0 + sc_ref[0]) + sh_ref[0]
    strips = []
    for i, (wga_ref, wgc_ref) in enumerate(((wga0_ref, wgc0_ref), (wga1_ref, wgc1_ref))):
        for j in range(W_IN_BLOCK // MIX_STRIP):
            loc = slice(j * MIX_STRIP, (j + 1) * MIX_STRIP)
            cols = slice(i * W_IN_BLOCK + j * MIX_STRIP, i * W_IN_BLOCK + (j + 1) * MIX_STRIP)
            strips.append(jax.nn.sigmoid(_dot(u, wga_ref[:, loc])) * _dot(at_ref[0], wa_ref[:, cols])
                          + jax.nn.sigmoid(_dot(u, wgc_ref[:, loc])) * _dot(ch_ref[0], wc_ref[:, cols]))
    y = _dot(jnp.concatenate(strips, axis=1), wo_ref[...])
    o_ref[0] = _layer_norm(alpha * x + (1.0 + gt_ref[0]) * y, lng_ref[...], lnb_ref[...])


def _mix_call(layer, alpha, x, ada, attn_t, conv_h, w_in, w_attn_br, w_conv_pw, w_out, ln_g, ln_b, w_up, w_down):
    b, s, d = x.shape
    tm = ROW_TILE
    n_s = s // tm
    n_steps = b * n_s
    d_ff = w_up.shape[2]
    assert d % (n_steps * BF16_ROWS) == 0 and d_ff % (n_steps * BF16_ROWS) == 0
    up_rows, down_rows = d // n_steps, d_ff // n_steps

    def ada_map(slot):
        return lambda bi, si: (_ada_row(layer, bi, slot), 0, 0)

    def whole(arr):
        return pl.BlockSpec((None,) + arr.shape[1:], lambda bi, si: (layer,) + (0,) * (arr.ndim - 1),
                            pipeline_mode=pl.Buffered(1))

    return pl.pallas_call(
        functools.partial(_mix_kernel, alpha),
        grid=(b, s // tm),
        in_specs=[
            pl.BlockSpec((1, tm, d), lambda bi, si: (bi, si, 0)),
            pl.BlockSpec((1, 1, d), ada_map(1)),
            pl.BlockSpec((1, 1, d), ada_map(0)),
            pl.BlockSpec((1, 1, d), ada_map(2)),
            pl.BlockSpec((1, tm, ATTN_WIDTH), lambda bi, si: (bi, si, 0)),
            pl.BlockSpec((1, tm, CONV_WIDTH), lambda bi, si: (bi, si, 0)),
            *[_w_in_block(layer, j) for j in range(5, 9)],
            whole(w_attn_br), whole(w_conv_pw), whole(w_out), whole(ln_g), whole(ln_b),
            pl.BlockSpec((None, up_rows, d_ff), lambda bi, si: (layer, bi * n_s + si, 0)),
            pl.BlockSpec((None, down_rows, d), lambda bi, si: (layer, bi * n_s + si, 0)),
        ],
        out_specs=[
            pl.BlockSpec((1, tm, d), lambda bi, si: (bi, si, 0)),
            pl.BlockSpec((up_rows, d_ff), lambda bi, si: (bi * n_s + si, 0)),
            pl.BlockSpec((down_rows, d), lambda bi, si: (bi * n_s + si, 0)),
        ],
        out_shape=[
            jax.ShapeDtypeStruct((b, s, d), F32),
            jax.ShapeDtypeStruct((d, d_ff), BF16),
            jax.ShapeDtypeStruct((d_ff, d), BF16),
        ],
        compiler_params=_params("parallel", "parallel"),
        name="mix_out",
    )(x, ada, ada, ada, attn_t, conv_h, w_in, w_in, w_in, w_in, w_attn_br, w_conv_pw, w_out, ln_g, ln_b,
      w_up, w_down)


def _ffn_kernel(alpha, x_ref, sc_ref, sh_ref, gt_ref, wu_ref, wd_ref, lng_ref, lnb_ref, o_ref):
    d = D_MODEL
    x = x_ref[0]
    u = (x * (1.0 + sc_ref[0]) + sh_ref[0]).astype(BF16)
    y = None
    for c in range(D_FF // d):
        h = jnp.maximum(_dot(u, wu_ref[:, c * d:(c + 1) * d]), 0.0)
        part = _dot((h * h).astype(BF16), wd_ref[c * d:(c + 1) * d, :])
        y = part if y is None else y + part
    o_ref[0] = _layer_norm(alpha * x + (1.0 + gt_ref[0]) * y, lng_ref[...], lnb_ref[...])


def _ffn_call(layer, alpha, x, ada, w_up_b, w_down_b, ln_g, ln_b):
    b, s, d = x.shape
    tm = ROW_TILE

    def ada_map(slot):
        return lambda bi, si: (_ada_row(layer, bi, slot), 0, 0)

    def whole(arr):
        return pl.BlockSpec((None,) + arr.shape[1:], lambda bi, si: (layer,) + (0,) * (arr.ndim - 1),
                            pipeline_mode=pl.Buffered(1))

    def resident(arr):
        return pl.BlockSpec(arr.shape, lambda bi, si: (0,) * arr.ndim, pipeline_mode=pl.Buffered(1))

    return pl.pallas_call(
        functools.partial(_ffn_kernel, alpha),
        grid=(b, s // tm),
        in_specs=[
            pl.BlockSpec((1, tm, d), lambda bi, si: (bi, si, 0)),
            pl.BlockSpec((1, 1, d), ada_map(4)),
            pl.BlockSpec((1, 1, d), ada_map(3)),
            pl.BlockSpec((1, 1, d), ada_map(5)),
            resident(w_up_b), resident(w_down_b), whole(ln_g), whole(ln_b),
        ],
        out_specs=pl.BlockSpec((1, tm, d), lambda bi, si: (bi, si, 0)),
        out_shape=jax.ShapeDtypeStruct((b, s, d), F32),
        compiler_params=_params("parallel", "parallel"),
        name="ffn",
    )(x, ada, ada, ada, w_up_b, w_down_b, ln_g, ln_b)


def _rope_tables(positions):
    inv_freq = ROPE_THETA ** (-jnp.arange(0, ROT_DIM, 2, dtype=F32) / ROT_DIM)
    ang = positions.astype(F32)[..., None] * inv_freq
    cos, sin = jnp.cos(ang), jnp.sin(ang)
    rest = HEAD_DIM - ROT_DIM
    cos_h = jnp.concatenate([cos, cos, jnp.ones(cos.shape[:-1] + (rest,), F32)], -1)
    sin_h = jnp.concatenate([-sin, sin, jnp.zeros(sin.shape[:-1] + (rest,), F32)], -1)
    return jnp.tile(cos_h, (1, 1, HEADS_PER_GROUP)), jnp.tile(sin_h, (1, 1, HEADS_PER_GROUP))


def kernel(x, c, positions, w_in, w_attn_br, conv_dw_w, conv_dw_b, conv_ln_g, conv_ln_b, w_conv_pw, w_out,
           w_ada, b_ada, ln_mix_g, ln_mix_b, w_up, w_down, ln_ffn_g, ln_ffn_b):
    depth = w_in.shape[0]
    b, s, d = x.shape
    assert d == D_MODEL and s % ROW_TILE == 0 and ROW_TILE % MOBA_BLOCK == 0 and b <= ADA_ROWS
    assert s % (Q_BLOCKS * MOBA_BLOCK) == 0 and s % INPROJ_TILE == 0 and INPROJ_TILE % MOBA_BLOCK == 0
    alpha = (2 * depth) ** 0.25
    assert w_in.shape[2] == 9 * W_IN_BLOCK and ATTN_WIDTH == CONV_WIDTH == W_IN_BLOCK and d == 2 * W_IN_BLOCK
    n_blk = s // MOBA_BLOCK

    c_pad = jnp.zeros((ADA_ROWS, d), F32).at[:b].set(c)
    ada = _ada_call(c_pad, w_ada, b_ada).reshape(depth * ADA_ROWS * ADA_SLOTS, 1, d)
    cosf, sinf = _rope_tables(positions)

    w_attn_br_b = w_attn_br.astype(BF16)
    w_conv_pw_b = w_conv_pw.astype(BF16)
    row = lambda t: t.reshape(depth, 1, t.shape[-1])
    dw_w_tiles = jnp.broadcast_to(conv_dw_w[:, :, None, :], (depth, CONV_KERNEL, SUBLANES, CONV_WIDTH))

    for layer in range(depth):
        q, k, vt, kmean, conv_h = _inproj_call(layer, x, ada, w_in, cosf, sinf, dw_w_tiles,
                                               row(conv_dw_b), row(conv_ln_g), row(conv_ln_b))
        attn_t = _attn_call(q, k.reshape(b, n_blk, MOBA_BLOCK, ATTN_WIDTH), vt,
                            kmean.reshape(b, n_blk, ATTN_WIDTH))
        x, w_up_b, w_down_b = _mix_call(layer, alpha, x, ada, attn_t, conv_h, w_in, w_attn_br_b, w_conv_pw_b, w_out,
                                        row(ln_mix_g), row(ln_mix_b), w_up, w_down)
        x = _ffn_call(layer, alpha, x, ada, w_up_b, w_down_b, row(ln_ffn_g), row(ln_ffn_b))
    return x
```
